```python
import jax, jax.numpy as jnp
from jax import lax
import numpy as np

D_MODEL = 1024
BATCH = 1
SEQ = 16384
DEPTH = 4
DEC_BATCH = 32
DEC_SEQ = 64
PAST_LEN = 4096

CHUNK = 64
N_HEADS = 8
HEAD_DIM = 64
ATTN_DIM = N_HEADS * HEAD_DIM
CONV_DIM = D_MODEL - ATTN_DIM
CONV_GROUPS = 8
CONV_W = 3
D_FF = 4 * D_MODEL
Q_BLOCK = 128
EPS = 1e-6
NEG_INF = -1e30
PROJ_DIM = 3 * ATTN_DIM + N_HEADS + 3 * CONV_DIM

kernel_name = "hymba_fox_shortconv_streaming_step"


def rms_norm(x, g):
    xf = x.astype(jnp.float32)
    y = xf * lax.rsqrt(jnp.mean(xf * xf, axis=-1, keepdims=True) + EPS)
    return (y * g.astype(jnp.float32)).astype(x.dtype)


def project(h, w_in, b_f):
    b, t = h.shape[0], h.shape[1]
    z = h @ w_in
    cuts = [ATTN_DIM, 2 * ATTN_DIM, 3 * ATTN_DIM, 3 * ATTN_DIM + N_HEADS,
            3 * ATTN_DIM + N_HEADS + CONV_DIM, 3 * ATTN_DIM + N_HEADS + 2 * CONV_DIM]
    q, k, v, f, gate_b, gate_c, u = jnp.split(z, cuts, axis=-1)
    q = q.reshape(b, t, N_HEADS, HEAD_DIM)
    k = k.reshape(b, t, N_HEADS, HEAD_DIM)
    v = v.reshape(b, t, N_HEADS, HEAD_DIM)
    logf = jax.nn.log_sigmoid((f + b_f).astype(jnp.float32))
    return q, k, v, logf, gate_b, gate_c * u


def fox_attend(q, k, v, cq, ck, pos_q, pos_k):
    scale = HEAD_DIM ** -0.5
    s = jnp.einsum('bqhd,bkhd->bhqk', q, k).astype(jnp.float32) * scale
    decay = jnp.transpose(cq, (0, 2, 1))[:, :, :, None] - jnp.transpose(ck, (0, 2, 1))[:, :, None, :]
    mask = pos_k[None, :] <= pos_q[:, None]
    s = jnp.where(mask[None, None], s + decay, NEG_INF)
    p = jax.nn.softmax(s, axis=-1)
    return jnp.einsum('bhqk,bkhd->bqhd', p.astype(v.dtype), v)


def fox_prompt(q, k, v, logf):
    b, t = q.shape[0], q.shape[1]
    c = jnp.cumsum(logf.astype(jnp.float32), axis=1)
    nb = t // Q_BLOCK
    qb = q.reshape(b, nb, Q_BLOCK, N_HEADS, HEAD_DIM).transpose(1, 0, 2, 3, 4)
    cb = c.reshape(b, nb, Q_BLOCK, N_HEADS).transpose(1, 0, 2, 3)
    pb = jnp.arange(t, dtype=jnp.int32).reshape(nb, Q_BLOCK)
    pos_k = jnp.arange(t, dtype=jnp.int32)
    out = lax.map(lambda a: fox_attend(a[0], k, v, a[1], c, a[2], pos_k), (qb, cb, pb))
    return out.transpose(1, 0, 2, 3, 4).reshape(b, t, N_HEADS, HEAD_DIM)


def fox_sample(q, k, v, logf, past_k, past_v, past_logf):
    p_len, t = past_k.shape[1], q.shape[1]
    k_all = jnp.concatenate([past_k, k], axis=1)
    v_all = jnp.concatenate([past_v, v], axis=1)
    logf_all = jnp.concatenate([past_logf.astype(jnp.float32), logf.astype(jnp.float32)], axis=1)
    c = jnp.cumsum(logf_all, axis=1)
    pos_k = jnp.arange(p_len + t, dtype=jnp.int32)
    pos_q = p_len + jnp.arange(t, dtype=jnp.int32)
    return fox_attend(q, k_all, v_all, c[:, p_len:], c, pos_q, pos_k)


def short_conv(cu, prev, w):
    t = cu.shape[1]
    up = jnp.concatenate([prev.astype(cu.dtype), cu], axis=1)
    y = w[0] * up[:, 0:t]
    for j in range(1, CONV_W):
        y = y + w[j] * up[:, j:j + t]
    return y, up[:, up.shape[1] - (CONV_W - 1):]


def trunk_layer(x, conv_prev, attn_fn, g_pre_mix, w_in, b_f, conv_w, g_attn_out, g_conv_out,
                w_o, g_post_mix, g_pre_mlp, w_up, w_down, g_post_mlp):
    b, t = x.shape[0], x.shape[1]
    h = rms_norm(x, g_pre_mix)
    q, k, v, logf, gate_b, cu = project(h, w_in, b_f)
    o_attn = attn_fn(q, k, v, logf).reshape(b, t, ATTN_DIM)
    yc, conv_state = short_conv(cu, conv_prev, conv_w)
    o_conv = gate_b * yc
    merged = jnp.concatenate([rms_norm(o_attn, g_attn_out), rms_norm(o_conv, g_conv_out)], axis=-1)
    x = x + rms_norm(merged @ w_o, g_post_mix)
    h2 = rms_norm(x, g_pre_mlp)
    m = jnp.square(jax.nn.relu(h2 @ w_up)) @ w_down
    x = x + rms_norm(m, g_post_mlp)
    return x, k, v, logf, conv_state


def setup_inputs(seed: int = 0) -> dict:
    key = jax.random.key(seed)
    ks = jax.random.split(key, 24)
    f32 = jnp.float32
    nrm = lambda k, s, sc: jax.random.normal(k, s, f32) * sc
    gain = lambda k, s: 1.0 + 0.05 * jax.random.normal(k, s, f32)
    return {
        "x_prompt": nrm(ks[0], (BATCH, SEQ, D_MODEL), 1.0),
        "x_sample": nrm(ks[1], (DEC_BATCH, DEC_SEQ, D_MODEL), 1.0),
        "cache_k": nrm(ks[2], (DEPTH, DEC_BATCH, PAST_LEN, N_HEADS, HEAD_DIM), 1.0),
        "cache_v": nrm(ks[3], (DEPTH, DEC_BATCH, PAST_LEN, N_HEADS, HEAD_DIM), 1.0),
        "cache_logf": jax.nn.log_sigmoid(jax.random.uniform(ks[4], (DEPTH, DEC_BATCH, PAST_LEN, N_HEADS), f32, 1.0, 5.0)
                                          + jax.random.normal(ks[5], (DEPTH, DEC_BATCH, PAST_LEN, N_HEADS), f32)),
        "state_conv": nrm(ks[6], (DEPTH, DEC_BATCH, CONV_W - 1, CONV_DIM), 1.0),
        "g_pre_mix": gain(ks[7], (DEPTH, D_MODEL)),
        "w_in": nrm(ks[8], (DEPTH, D_MODEL, PROJ_DIM), D_MODEL ** -0.5),
        "b_f": jax.random.uniform(ks[9], (DEPTH, N_HEADS), f32, 1.0, 5.0),
        "conv_w": nrm(ks[10], (DEPTH, CONV_W, CONV_DIM), CONV_W ** -0.5),
        "g_attn_out": gain(ks[11], (DEPTH, ATTN_DIM)),
        "g_conv_out": gain(ks[12], (DEPTH, CONV_DIM)),
        "w_o": nrm(ks[13], (DEPTH, D_MODEL, D_MODEL), D_MODEL ** -0.5),
        "g_post_mix": gain(ks[14], (DEPTH, D_MODEL)),
        "g_pre_mlp": gain(ks[15], (DEPTH, D_MODEL)),
        "w_up": nrm(ks[16], (DEPTH, D_MODEL, D_FF), D_MODEL ** -0.5),
        "w_down": nrm(ks[17], (DEPTH, D_FF, D_MODEL), D_FF ** -0.5),
        "g_post_mlp": gain(ks[18], (DEPTH, D_MODEL)),
    }


def reference(x_prompt, x_sample, cache_k, cache_v, cache_logf, state_conv,
              g_pre_mix, w_in, b_f, conv_w, g_attn_out, g_conv_out, w_o,
              g_post_mix, g_pre_mlp, w_up, w_down, g_post_mlp):
    xp, xs = x_prompt, x_sample
    zero_conv = jnp.zeros((x_prompt.shape[0], CONV_W - 1, CONV_DIM), x_prompt.dtype)
    kp, vp, fp, cp = [], [], [], []
    ksl, vsl, fsl, csl = [], [], [], []
    for l in range(DEPTH):
        params = (g_pre_mix[l], w_in[l], b_f[l], conv_w[l], g_attn_out[l], g_conv_out[l],
                  w_o[l], g_post_mix[l], g_pre_mlp[l], w_up[l], w_down[l], g_post_mlp[l])
        xp, k1, v1, f1, c1 = trunk_layer(xp, zero_conv, fox_prompt, *params)
        pk, pv, pf = cache_k[l], cache_v[l], cache_logf[l]
        samp_fn = lambda q, k, v, lf, pk=pk, pv=pv, pf=pf: fox_sample(q, k, v, lf, pk, pv, pf)
        xs, k2, v2, f2, c2 = trunk_layer(xs, state_conv[l], samp_fn, *params)
        kp.append(k1); vp.append(v1); fp.append(f1); cp.append(c1)
        ksl.append(k2); vsl.append(v2); fsl.append(f2); csl.append(c2)
    return (xp, xs,
            jnp.stack(kp), jnp.stack(vp), jnp.stack(fp), jnp.stack(cp),
            jnp.stack(ksl), jnp.stack(vsl), jnp.stack(fsl), jnp.stack(csl))
```

```python
import functools
import math

import numpy as np
import jax
import jax.numpy as jnp
from jax import lax
from jax.experimental import pallas as pl
from jax.experimental.pallas import tpu as pltpu

N_HEADS = 8
HEAD_DIM = 64
ATTN_DIM = N_HEADS * HEAD_DIM
CONV_W = 3
EPS = 1e-6
NEG_INF = -1e30
LOG2E = math.log2(math.e)

LANES = 128
SUBLANES = 8
VMEM_LIMIT_BYTES = 56 * 1024 * 1024

TOKEN_TILE = 512
ATTN_BLOCK = 512
CUMSUM_CHUNK = 512
FF_CHUNK = 1024

F32 = jnp.float32
BF16 = jnp.bfloat16

AUG_Q0 = HEAD_DIM
AUG_K0 = HEAD_DIM
ONES_LANE_V = HEAD_DIM
CP_ONE_LANE = 3 * N_HEADS


def _split3(x):
    hi = x.astype(BF16).astype(F32)
    r = x - hi
    mid = r.astype(BF16).astype(F32)
    lo = (r - mid).astype(BF16).astype(F32)
    return hi, mid, lo


def _rms(x, g):
    return x * lax.rsqrt(jnp.mean(x * x, axis=-1, keepdims=True) + EPS) * g


def _log_sigmoid(x):
    return jnp.minimum(x, 0.0) - jnp.log1p(jnp.exp(-jnp.abs(x)))


def _dot(a, b):
    return jnp.dot(a, b, preferred_element_type=F32)


def _dot_nt(a, b):
    return lax.dot_general(a, b, (((1,), (1,)), ((), ())), preferred_element_type=F32)


def _placement_matrix():
    p = np.zeros((LANES, 2 * N_HEADS * LANES), np.float32)
    for h in range(N_HEADS):
        for j in range(3):
            p[j * N_HEADS + h, h * LANES + AUG_Q0 + j] = 1.0
            p[CP_ONE_LANE, h * LANES + AUG_Q0 + 3 + j] = 1.0
            p[CP_ONE_LANE, N_HEADS * LANES + h * LANES + AUG_K0 + j] = 1.0
            p[j * N_HEADS + h, N_HEADS * LANES + h * LANES + AUG_K0 + 3 + j] = -1.0
    return jnp.asarray(p, BF16)


def _proj_kernel(*refs, tm, nseq, tiles_per_seq, prompt):
    if prompt:
        (x_ref, g_ref, w_ref, bf_ref, cw_ref, gco_ref, st_ref, tri_ref, pm_ref,
         k_ref, v_ref, lf_ref, ocn_ref, nst_ref, qa_ref, ka_ref, va_ref,
         cu_scr, cc_scr) = refs
    else:
        (x_ref, g_ref, w_ref, bf_ref, cw_ref, gco_ref, st_ref,
         k_ref, v_ref, lf_ref, ocn_ref, nst_ref, qa_ref,
         cu_scr) = refs
    i = pl.program_id(0)
    d_attn = ATTN_DIM
    d_conv = cw_ref.shape[1]

    h = _rms(x_ref[...], g_ref[...]).astype(BF16)
    z = _dot(h, w_ref[...])
    zq = z[:, 0:d_attn] * (HEAD_DIM ** -0.5 * LOG2E)
    zk = z[:, d_attn:2 * d_attn]
    zv = z[:, 2 * d_attn:3 * d_attn]
    o = 3 * d_attn
    gate_b = z[:, o:o + d_conv]
    cu = z[:, o + d_conv:o + 2 * d_conv] * z[:, o + 2 * d_conv:o + 3 * d_conv]
    fz = z[:, o + 3 * d_conv:o + 3 * d_conv + LANES] + bf_ref[...]

    k_ref[...] = zk
    v_ref[...] = zv
    lane = lax.broadcasted_iota(jnp.int32, (tm, LANES), 1)
    logf = jnp.where(lane < N_HEADS, _log_sigmoid(fz), 0.0)
    lf_ref[...] = logf

    seg = tm // nseq
    slot = seg + SUBLANES
    if nseq == 1:
        first = (i % tiles_per_seq) == 0

        @pl.when(first)
        def _():
            cu_scr[6:8, :] = st_ref[0]

        @pl.when(jnp.logical_not(first))
        def _():
            cu_scr[6:8, :] = cu_scr[seg + 6:seg + 8, :]

        cu_scr[8:8 + seg, :] = cu
        cu_m1 = cu_scr[7:7 + seg, :]
        cu_m2 = cu_scr[6:6 + seg, :]
        nst_ref[0] = cu_scr[seg + 6:seg + 8, :]
    else:
        for s in range(nseq):
            cu_scr[s * slot + 6:s * slot + 8, :] = st_ref[s]
            cu_scr[s * slot + 8:s * slot + 8 + seg, :] = cu[s * seg:(s + 1) * seg, :]
        cu_m1 = jnp.concatenate([cu_scr[s * slot + 7:s * slot + 7 + seg, :] for s in range(nseq)], axis=0)
        cu_m2 = jnp.concatenate([cu_scr[s * slot + 6:s * slot + 6 + seg, :] for s in range(nseq)], axis=0)
        for s in range(nseq):
            nst_ref[s] = cu_scr[s * slot + seg + 6:s * slot + seg + 8, :]
    cw = cw_ref[...]
    y = cw[0:1, :] * cu_m2 + cw[1:2, :] * cu_m1 + cw[2:3, :] * cu
    ocn_ref[...] = _rms(gate_b * y, gco_ref[...]).astype(BF16)

    if prompt:
        tri = tri_ref[...]
        hi, mid, lo = _split3(logf)
        loc = _dot(tri, hi.astype(BF16)) + _dot(tri, mid.astype(BF16)) + _dot(tri, lo.astype(BF16))

        @pl.when((i % tiles_per_seq) == 0)
        def _():
            cc_scr[...] = jnp.zeros_like(cc_scr)

        c = loc + cc_scr[SUBLANES - 1:SUBLANES, :]
        cc_scr[...] = c[tm - SUBLANES:tm, :]
        chi, cmid, clo = _split3(c * LOG2E)
        cp = (chi + pltpu.roll(cmid, N_HEADS, 1) + pltpu.roll(clo, 2 * N_HEADS, 1)
              + (lane == CP_ONE_LANE).astype(F32))
        aug = _dot(cp.astype(BF16), pm_ref[...])
        ones_v = (lane == ONES_LANE_V).astype(F32)

    low = lane < HEAD_DIM
    for j in range(N_HEADS // 2):
        qb = zq[:, j * LANES:(j + 1) * LANES]
        if prompt:
            kb = zk[:, j * LANES:(j + 1) * LANES]
            vb = zv[:, j * LANES:(j + 1) * LANES]
        for par in range(2):
            hd = 2 * j + par
            if prompt:
                qs, ks, vs = (qb, kb, vb) if par == 0 else (
                    pltpu.roll(qb, HEAD_DIM, 1), pltpu.roll(kb, HEAD_DIM, 1), pltpu.roll(vb, HEAD_DIM, 1))
                qa_ref[hd] = jnp.where(low, qs, aug[:, hd * LANES:(hd + 1) * LANES]).astype(BF16)
                ka_ref[hd] = jnp.where(
                    low, ks, aug[:, (N_HEADS + hd) * LANES:(N_HEADS + hd + 1) * LANES]).astype(BF16)
                va_ref[hd] = jnp.where(low, vs, ones_v).astype(BF16)
            else:
                keep = low if par == 0 else jnp.logical_not(low)
                qa_ref[hd] = jnp.where(keep, qb, 0.0).astype(BF16)


def _proj(x, layer, g_pre_mix, w1, bf_pad, conv_w, g_conv_out, state, *, seq_len, prompt):
    n_tok, d = x.shape
    d_conv = conv_w.shape[-1]
    n_cols = w1.shape[-1]
    tm = min(TOKEN_TILE, n_tok)
    assert n_tok % tm == 0
    if seq_len >= tm:
        assert seq_len % tm == 0
        nseq, tiles_per_seq = 1, seq_len // tm
        st_map = lambda i: (i // tiles_per_seq, 0, 0)
    else:
        assert tm % seq_len == 0 and seq_len % SUBLANES == 0
        nseq, tiles_per_seq = tm // seq_len, 1
        st_map = lambda i: (i, 0, 0)
    seg = tm // nseq
    n_tiles = n_tok // tm
    row = lambda i: (i, 0)
    lay2 = lambda i: (layer, 0, 0)
    in_specs = [
        pl.BlockSpec((tm, d), row),
        pl.BlockSpec((None, 1, d), lay2),
        pl.BlockSpec((None, d, n_cols), lay2),
        pl.BlockSpec((None, 1, LANES), lay2),
        pl.BlockSpec((None, CONV_W, d_conv), lay2),
        pl.BlockSpec((None, 1, d_conv), lay2),
        pl.BlockSpec((nseq, CONV_W - 1, d_conv), st_map),
    ]
    args = [x, g_pre_mix, w1, bf_pad, conv_w, g_conv_out, state]
    head_spec = pl.BlockSpec((N_HEADS, tm, LANES), lambda i: (0, i, 0))
    out_specs = [
        pl.BlockSpec((tm, ATTN_DIM), row),
        pl.BlockSpec((tm, ATTN_DIM), row),
        pl.BlockSpec((tm, LANES), row),
        pl.BlockSpec((tm, d_conv), row),
        pl.BlockSpec((nseq, CONV_W - 1, d_conv), st_map),
        head_spec,
    ]
    out_shape = [
        jax.ShapeDtypeStruct((n_tok, ATTN_DIM), F32),
        jax.ShapeDtypeStruct((n_tok, ATTN_DIM), F32),
        jax.ShapeDtypeStruct((n_tok, LANES), F32),
        jax.ShapeDtypeStruct((n_tok, d_conv), BF16),
        jax.ShapeDtypeStruct(state.shape, F32),
        jax.ShapeDtypeStruct((N_HEADS, n_tok, LANES), BF16),
    ]
    scratch = [pltpu.VMEM((nseq * (seg + SUBLANES), d_conv), F32)]
    if prompt:
        tri = jnp.tril(jnp.ones((tm, tm), BF16))
        in_specs += [pl.BlockSpec((tm, tm), lambda i: (0, 0)),
                     pl.BlockSpec((LANES, 2 * N_HEADS * LANES), lambda i: (0, 0))]
        args += [tri, _placement_matrix()]
        out_specs += [head_spec, head_spec]
        out_shape += [jax.ShapeDtypeStruct((N_HEADS, n_tok, LANES), BF16)] * 2
        scratch += [pltpu.VMEM((SUBLANES, LANES), F32)]
    kern = functools.partial(_proj_kernel, tm=tm, nseq=nseq, tiles_per_seq=tiles_per_seq, prompt=prompt)
    return pl.pallas_call(
        kern,
        grid=(n_tiles,),
        in_specs=in_specs,
        out_specs=out_specs,
        out_shape=out_shape,
        scratch_shapes=scratch,
        compiler_params=pltpu.CompilerParams(
            dimension_semantics=("arbitrary",), vmem_limit_bytes=VMEM_LIMIT_BYTES),
        name="proj_prompt" if prompt else "proj_sample",
    )(*args)


def _attn_prompt_kernel(q_ref, k_ref, v_ref, o_ref, m_scr, acc_scr, *, blk, nb):
    i = pl.program_id(1) % nb
    row = lax.broadcasted_iota(jnp.int32, (blk, blk), 0)
    col = lax.broadcasted_iota(jnp.int32, (blk, blk), 1)
    lane = lax.broadcasted_iota(jnp.int32, (blk, LANES), 1)
    outs = []
    for hh in range(2):
        q = q_ref[hh]
        m_scr[...] = jnp.full(m_scr.shape, NEG_INF, F32)
        acc_scr[...] = jnp.zeros_like(acc_scr)

        def step(j, diagonal, hh=hh, q=q):
            start = pl.multiple_of(j * blk, blk)
            k = k_ref[hh, pl.ds(start, blk), :]
            v = v_ref[hh, pl.ds(start, blk), :]
            s = _dot_nt(q, k)
            if diagonal:
                s = jnp.where(col <= row, s, NEG_INF)
            m_old = m_scr[...]
            m_new = jnp.maximum(m_old, jnp.max(s, axis=-1, keepdims=True))
            alpha = jnp.exp2(m_old - m_new)
            p = jnp.exp2(s - m_new)
            acc_scr[...] = alpha * acc_scr[...] + _dot(p.astype(BF16), v)
            m_scr[...] = m_new

        def body(j, carry):
            step(j, False)
            return carry

        lax.fori_loop(0, i, body, 0)
        step(i, True)
        acc = acc_scr[...]
        outs.append(acc / acc[:, ONES_LANE_V:ONES_LANE_V + 1])
    o_ref[...] = jnp.where(lane < HEAD_DIM, outs[0], pltpu.roll(outs[1], HEAD_DIM, 1))


def _attn_prompt(qa, ka, va, *, seq_len):
    n_tok = qa.shape[1]
    blk = min(ATTN_BLOCK, seq_len)
    assert seq_len % blk == 0 and n_tok % seq_len == 0
    n_seq = n_tok // seq_len
    nb = seq_len // blk
    kv_spec = pl.BlockSpec((2, seq_len, LANES), lambda p, i: (p, i // nb, 0))
    return pl.pallas_call(
        functools.partial(_attn_prompt_kernel, blk=blk, nb=nb),
        grid=(N_HEADS // 2, n_seq * nb),
        in_specs=[pl.BlockSpec((2, blk, LANES), lambda p, i: (p, i, 0)), kv_spec, kv_spec],
        out_specs=pl.BlockSpec((blk, LANES), lambda p, i: (i, p)),
        out_shape=jax.ShapeDtypeStruct((n_tok, ATTN_DIM), F32),
        scratch_shapes=[pltpu.VMEM((blk, 1), F32), pltpu.VMEM((blk, LANES), F32)],
        compiler_params=pltpu.CompilerParams(
            dimension_semantics=("arbitrary", "arbitrary"), vmem_limit_bytes=VMEM_LIMIT_BYTES),
        name="attn_prompt",
    )(qa, ka, va)


def _attn_sample_kernel(qz_ref, kn_ref, vn_ref, kc_ref, vc_ref, lft_ref, lf_ref, u_ref, o_ref, *, t_new, p_len):
    n_ext = lft_ref.shape[-1]
    n_chunks = n_ext // CUMSUM_CHUNK
    u = u_ref[...]

    x = lft_ref[0]
    hi, mid, lo = _split3(x)
    x3 = jnp.concatenate([hi, mid, lo, jnp.zeros_like(hi)], axis=0).astype(BF16)
    offs = jnp.zeros((N_HEADS, 1), F32)
    pieces = []
    for c in range(n_chunks):
        loc3 = _dot(x3[:, c * CUMSUM_CHUNK:(c + 1) * CUMSUM_CHUNK], u)
        loc = loc3[0:N_HEADS] + loc3[N_HEADS:2 * N_HEADS] + loc3[2 * N_HEADS:3 * N_HEADS]
        pieces.append(loc + offs)
        offs = offs + loc[:, CUMSUM_CHUNK - 1:CUMSUM_CHUNK]
    c_abs = jnp.concatenate(pieces, axis=1)
    p_tot = c_abs[:, p_len - 1:p_len]
    r2 = (p_tot - c_abs) * LOG2E

    rown = lax.broadcasted_iota(jnp.int32, (t_new, t_new), 0)
    coln = lax.broadcasted_iota(jnp.int32, (t_new, t_new), 1)
    tri = (coln <= rown).astype(BF16)
    nhi, nmid, nlo = _split3(lf_ref[...])
    cn2 = (_dot(tri, nhi.astype(BF16)) + _dot(tri, nmid.astype(BF16)) + _dot(tri, nlo.astype(BF16))) * LOG2E

    kc = kc_ref[0].astype(BF16)
    vc = vc_ref[0].astype(BF16)
    kn = kn_ref[...].astype(BF16)
    vn = vn_ref[...].astype(BF16)
    lane = lax.broadcasted_iota(jnp.int32, (t_new, LANES), 1)
    for j in range(N_HEADS // 2):
        sl = slice(j * LANES, (j + 1) * LANES)
        res = []
        for par in range(2):
            hd = 2 * j + par
            q = qz_ref[hd]
            cq = cn2[:, hd:hd + 1]
            sp = _dot_nt(q, kc[:, sl]) + cq + r2[hd:hd + 1, 0:p_len]
            sn = _dot_nt(q, kn[:, sl]) + cq + r2[hd:hd + 1, p_len:p_len + t_new]
            sn = jnp.where(coln <= rown, sn, NEG_INF)
            m = jnp.maximum(jnp.max(sp, axis=-1, keepdims=True), jnp.max(sn, axis=-1, keepdims=True))
            pp = jnp.exp2(sp - m)
            pn = jnp.exp2(sn - m)
            l = jnp.sum(pp, axis=-1, keepdims=True) + jnp.sum(pn, axis=-1, keepdims=True)
            res.append((_dot(pp.astype(BF16), vc[:, sl]) + _dot(pn.astype(BF16), vn[:, sl])) / l)
        o_ref[:, sl] = jnp.where(lane < HEAD_DIM, res[0], res[1])


def _attn_sample(qz, k_new, v_new, cache_k, cache_v, lft_ext, lf_pad, layer, *, t_new):
    n_tok = k_new.shape[0]
    n_b = n_tok // t_new
    p_len = cache_k.shape[2]
    n_ext = lft_ext.shape[-1]
    assert p_len % CUMSUM_CHUNK == 0 and n_ext % CUMSUM_CHUNK == 0 and t_new % SUBLANES == 0
    u = jnp.triu(jnp.ones((CUMSUM_CHUNK, CUMSUM_CHUNK), BF16))
    tok = lambda b: (b, 0)
    cache_spec = pl.BlockSpec((None, 1, p_len, ATTN_DIM), lambda b: (layer, b, 0, 0))
    return pl.pallas_call(
        functools.partial(_attn_sample_kernel, t_new=t_new, p_len=p_len),
        grid=(n_b,),
        in_specs=[
            pl.BlockSpec((N_HEADS, t_new, LANES), lambda b: (0, b, 0)),
            pl.BlockSpec((t_new, ATTN_DIM), tok),
            pl.BlockSpec((t_new, ATTN_DIM), tok),
            cache_spec, cache_spec,
            pl.BlockSpec((1, N_HEADS, n_ext), lambda b: (b, 0, 0)),
            pl.BlockSpec((t_new, LANES), tok),
            pl.BlockSpec((CUMSUM_CHUNK, CUMSUM_CHUNK), lambda b: (0, 0)),
        ],
        out_specs=pl.BlockSpec((t_new, ATTN_DIM), tok),
        out_shape=jax.ShapeDtypeStruct((n_tok, ATTN_DIM), F32),
        compiler_params=pltpu.CompilerParams(
            dimension_semantics=("arbitrary",), vmem_limit_bytes=VMEM_LIMIT_BYTES),
        name="attn_sample",
    )(qz, k_new, v_new, cache_k, cache_v, lft_ext, lf_pad, u)


def _post_kernel(x_ref, oa_ref, ocn_ref, gao_ref, wo_ref, gpm_ref, gpl_ref, wup_ref, wdn_ref, gpo_ref, y_ref):
    d_ff = wup_ref.shape[1]
    oan = _rms(oa_ref[...], gao_ref[...]).astype(BF16)
    merged = jnp.concatenate([oan, ocn_ref[...]], axis=-1)
    x1 = x_ref[...] + _rms(_dot(merged, wo_ref[...]), gpm_ref[...])
    h2 = _rms(x1, gpl_ref[...]).astype(BF16)
    m = None
    for c in range(d_ff // FF_CHUNK):
        sl = slice(c * FF_CHUNK, (c + 1) * FF_CHUNK)
        a = jnp.square(jnp.maximum(_dot(h2, wup_ref[:, sl]), 0.0)).astype(BF16)
        part = _dot(a, wdn_ref[sl, :])
        m = part if m is None else m + part
    y_ref[...] = x1 + _rms(m, gpo_ref[...])


def _post(x, oa, ocn, layer, g_attn_out, w_o, g_post_mix, g_pre_mlp, w_up, w_down, g_post_mlp):
    n_tok, d = x.shape
    d_ff = w_up.shape[-1]
    tm = min(TOKEN_TILE, n_tok)
    assert n_tok % tm == 0 and d_ff % FF_CHUNK == 0
    row = lambda i: (i, 0)
    lay = lambda i: (layer, 0, 0)
    vec = lambda n: pl.BlockSpec((None, 1, n), lay)
    return pl.pallas_call(
        _post_kernel,
        grid=(n_tok // tm,),
        in_specs=[
            pl.BlockSpec((tm, d), row),
            pl.BlockSpec((tm, oa.shape[1]), row),
            pl.BlockSpec((tm, ocn.shape[1]), row),
            vec(oa.shape[1]),
            pl.BlockSpec((None, d, d), lay),
            vec(d), vec(d),
            pl.BlockSpec((None, d, d_ff), lay),
            pl.BlockSpec((None, d_ff, d), lay),
            vec(d),
        ],
        out_specs=pl.BlockSpec((tm, d), row),
        out_shape=jax.ShapeDtypeStruct((n_tok, d), F32),
        compiler_params=pltpu.CompilerParams(
            dimension_semantics=("arbitrary",), vmem_limit_bytes=VMEM_LIMIT_BYTES),
        name="post",
    )(x, oa, ocn, g_attn_out, w_o, g_post_mix, g_pre_mlp, w_up, w_down, g_post_mlp)


def kernel(x_prompt, x_sample, cache_k, cache_v, cache_logf, state_conv, g_pre_mix, w_in, b_f, conv_w,
           g_attn_out, g_conv_out, w_o, g_post_mix, g_pre_mlp, w_up, w_down, g_post_mlp):
    depth = w_in.shape[0]
    batch, seq, d = x_prompt.shape
    dec_batch, dec_seq, _ = x_sample.shape
    past_len = cache_k.shape[2]
    d_conv = conv_w.shape[-1]
    assert cache_k.shape[3:] == (N_HEADS, HEAD_DIM) and d_conv + ATTN_DIM == d

    c0, c1 = 3 * ATTN_DIM, 3 * ATTN_DIM + N_HEADS
    w1 = jnp.concatenate(
        [w_in[:, :, :c0], w_in[:, :, c1:], w_in[:, :, c0:c1],
         jnp.zeros((depth, d, LANES - N_HEADS), w_in.dtype)], axis=-1).astype(BF16)
    bf_pad = jnp.pad(b_f, ((0, 0), (0, LANES - N_HEADS)))[:, None, :]
    w_o_b, w_up_b, w_dn_b = w_o.astype(BF16), w_up.astype(BF16), w_down.astype(BF16)
    vec3 = lambda a: a[:, None, :]
    g_pre_mix3, g_conv_out3, g_attn_out3 = vec3(g_pre_mix), vec3(g_conv_out), vec3(g_attn_out)
    g_post_mix3, g_pre_mlp3, g_post_mlp3 = vec3(g_post_mix), vec3(g_pre_mlp), vec3(g_post_mlp)

    ck = cache_k.reshape(depth, dec_batch, past_len, ATTN_DIM)
    cv = cache_v.reshape(depth, dec_batch, past_len, ATTN_DIM)
    past_lft = jnp.swapaxes(cache_logf, 2, 3)
    n_ext = past_len + CUMSUM_CHUNK * pl.cdiv(dec_seq, CUMSUM_CHUNK)
    zero_state = jnp.zeros((batch, CONV_W - 1, d_conv), F32)

    xp = x_prompt.reshape(batch * seq, d)
    xs = x_sample.reshape(dec_batch * dec_seq, d)
    outs = [[] for _ in range(8)]
    for l in range(depth):
        kp, vp, lfp, ocn_p, cst_p, qa, ka, va = _proj(
            xp, l, g_pre_mix3, w1, bf_pad, conv_w, g_conv_out3, zero_state, seq_len=seq, prompt=True)
        ks, vs, lfs, ocn_s, cst_s, qz = _proj(
            xs, l, g_pre_mix3, w1, bf_pad, conv_w, g_conv_out3, state_conv[l], seq_len=dec_seq, prompt=False)
        oa_p = _attn_prompt(qa, ka, va, seq_len=seq)
        new_lft = jnp.swapaxes(lfs[:, :N_HEADS].reshape(dec_batch, dec_seq, N_HEADS), 1, 2)
        lft_ext = jnp.concatenate(
            [past_lft[l], new_lft, jnp.zeros((dec_batch, N_HEADS, n_ext - past_len - dec_seq), F32)], axis=-1)
        oa_s = _attn_sample(qz, ks, vs, ck, cv, lft_ext, lfs, l, t_new=dec_seq)
        xp = _post(xp, oa_p, ocn_p, l, g_attn_out3, w_o_b, g_post_mix3, g_pre_mlp3, w_up_b, w_dn_b, g_post_mlp3)
        xs = _post(xs, oa_s, ocn_s, l, g_attn_out3, w_o_b, g_post_mix3, g_pre_mlp3, w_up_b, w_dn_b, g_post_mlp3)
        for lst, val in zip(outs, (
                kp.reshape(batch, seq, N_HEADS, HEAD_DIM), vp.reshape(batch, seq, N_HEADS, HEAD_DIM),
                lfp[:, :N_HEADS].reshape(batch, seq, N_HEADS), cst_p,
                ks.reshape(dec_batch, dec_seq, N_HEADS, HEAD_DIM), vs.reshape(dec_batch, dec_seq, N_HEADS, HEAD_DIM),
                lfs[:, :N_HEADS].reshape(dec_batch, dec_seq, N_HEADS), cst_s)):
            lst.append(val)
    return (xp.reshape(batch, seq, d), xs.reshape(dec_batch, dec_seq, d)) + tuple(jnp.stack(o) for o in outs)
```

```python
import functools
import math

import numpy as np
import jax
import jax.numpy as jnp
from jax import lax
from jax.experimental import pallas as pl
from jax.experimental.pallas import tpu as pltpu

N_HEADS = 8
HEAD_DIM = 64
ATTN_DIM = N_HEADS * HEAD_DIM
CONV_W = 3
EPS = 1e-6
NEG_INF = -1e30
LOG2E = math.log2(math.e)

LANES = 128
SUBLANES = 8
VMEM_LIMIT_BYTES = 56 * 1024 * 1024

TOKEN_TILE = 512
ATTN_QBLOCK = 1024
CUMSUM_CHUNK = 512
FF_CHUNK = 1024

F32 = jnp.float32
BF16 = jnp.bfloat16

AUG_Q0 = HEAD_DIM
AUG_K0 = HEAD_DIM
ONES_LANE_V = HEAD_DIM
CP_ONE_LANE = 3 * N_HEADS

STAT_QNORM, STAT_KNORM, STAT_CFIRST, STAT_CLAST, N_STATS = 0, 1, 2, 3, 4
NORM_MARGIN = 1.0 + 2.0 ** -6
SKIP_LOG2 = 160.0


def _split3(x):
    hi = x.astype(BF16).astype(F32)
    r = x - hi
    mid = r.astype(BF16).astype(F32)
    lo = (r - mid).astype(BF16).astype(F32)
    return hi, mid, lo


def _rms(x, g):
    return x * lax.rsqrt(jnp.mean(x * x, axis=-1, keepdims=True) + EPS) * g


def _log_sigmoid(x):
    return jnp.minimum(x, 0.0) - jnp.log1p(jnp.exp(-jnp.abs(x)))


def _dot(a, b):
    return jnp.dot(a, b, preferred_element_type=F32)


def _dot_nt(a, b):
    return lax.dot_general(a, b, (((1,), (1,)), ((), ())), preferred_element_type=F32)


def _placement_matrix():
    p = np.zeros((LANES, 2 * N_HEADS * LANES), np.float32)
    for h in range(N_HEADS):
        for j in range(3):
            p[j * N_HEADS + h, h * LANES + AUG_Q0 + j] = 1.0
            p[CP_ONE_LANE, h * LANES + AUG_Q0 + 3 + j] = 1.0
            p[CP_ONE_LANE, N_HEADS * LANES + h * LANES + AUG_K0 + j] = 1.0
            p[j * N_HEADS + h, N_HEADS * LANES + h * LANES + AUG_K0 + 3 + j] = -1.0
    return jnp.asarray(p, BF16)


def _proj_kernel(*refs, tm, nseq, tiles_per_seq, prompt):
    if prompt:
        (x_ref, g_ref, w_ref, bf_ref, cw_ref, gco_ref, st_ref, tri_ref, pm_ref, sel_ref,
         k_ref, v_ref, lf_ref, ocn_ref, nst_ref, qa_ref, ka_ref, va_ref, stat_ref,
         cu_scr, cc_scr) = refs
    else:
        (x_ref, g_ref, w_ref, bf_ref, cw_ref, gco_ref, st_ref,
         k_ref, v_ref, lf_ref, ocn_ref, nst_ref, qa_ref,
         cu_scr) = refs
    i = pl.program_id(0)
    d_attn = ATTN_DIM
    d_conv = cw_ref.shape[1]

    h = _rms(x_ref[...], g_ref[...]).astype(BF16)
    z = _dot(h, w_ref[...])
    zq = z[:, 0:d_attn] * (HEAD_DIM ** -0.5 * LOG2E)
    zk = z[:, d_attn:2 * d_attn]
    zv = z[:, 2 * d_attn:3 * d_attn]
    o = 3 * d_attn
    gate_b = z[:, o:o + d_conv]
    cu = z[:, o + d_conv:o + 2 * d_conv] * z[:, o + 2 * d_conv:o + 3 * d_conv]
    fz = z[:, o + 3 * d_conv:o + 3 * d_conv + LANES] + bf_ref[...]

    k_ref[...] = zk
    v_ref[...] = zv
    lane = lax.broadcasted_iota(jnp.int32, (tm, LANES), 1)
    logf = jnp.where(lane < N_HEADS, _log_sigmoid(fz), 0.0)
    lf_ref[...] = logf

    seg = tm // nseq
    slot = seg + SUBLANES
    if nseq == 1:
        first = (i % tiles_per_seq) == 0

        @pl.when(first)
        def _():
            cu_scr[6:8, :] = st_ref[0]

        @pl.when(jnp.logical_not(first))
        def _():
            cu_scr[6:8, :] = cu_scr[seg + 6:seg + 8, :]

        cu_scr[8:8 + seg, :] = cu
        cu_m1 = cu_scr[7:7 + seg, :]
        cu_m2 = cu_scr[6:6 + seg, :]
        nst_ref[0] = cu_scr[seg + 6:seg + 8, :]
    else:
        for s in range(nseq):
            cu_scr[s * slot + 6:s * slot + 8, :] = st_ref[s]
            cu_scr[s * slot + 8:s * slot + 8 + seg, :] = cu[s * seg:(s + 1) * seg, :]
        cu_m1 = jnp.concatenate([cu_scr[s * slot + 7:s * slot + 7 + seg, :] for s in range(nseq)], axis=0)
        cu_m2 = jnp.concatenate([cu_scr[s * slot + 6:s * slot + 6 + seg, :] for s in range(nseq)], axis=0)
        for s in range(nseq):
            nst_ref[s] = cu_scr[s * slot + seg + 6:s * slot + seg + 8, :]
    cw = cw_ref[...]
    y = cw[0:1, :] * cu_m2 + cw[1:2, :] * cu_m1 + cw[2:3, :] * cu
    ocn_ref[...] = _rms(gate_b * y, gco_ref[...]).astype(BF16)

    if prompt:
        tri = tri_ref[...]
        hi, mid, lo = _split3(logf)
        loc = _dot(tri, hi.astype(BF16)) + _dot(tri, mid.astype(BF16)) + _dot(tri, lo.astype(BF16))

        @pl.when((i % tiles_per_seq) == 0)
        def _():
            cc_scr[...] = jnp.zeros_like(cc_scr)

        c = loc + cc_scr[SUBLANES - 1:SUBLANES, :]
        cc_scr[...] = c[tm - SUBLANES:tm, :]
        c2 = c * LOG2E
        chi, cmid, clo = _split3(c2)
        cp = (chi + pltpu.roll(cmid, N_HEADS, 1) + pltpu.roll(clo, 2 * N_HEADS, 1)
              + (lane == CP_ONE_LANE).astype(F32))
        aug = _dot(cp.astype(BF16), pm_ref[...])
        ones_v = (lane == ONES_LANE_V).astype(F32)

        def max_norm(zz):
            zr = zz.astype(BF16).astype(F32)
            sumsq = _dot((zr * zr).astype(BF16), sel_ref[...])
            return jnp.sqrt(jnp.max(sumsq, axis=0, keepdims=True)) * NORM_MARGIN

        srow = lax.broadcasted_iota(jnp.int32, (SUBLANES, LANES), 0)
        stat_ref[0] = jnp.where(
            srow == STAT_QNORM, max_norm(zq), jnp.where(
                srow == STAT_KNORM, max_norm(zk), jnp.where(
                    srow == STAT_CFIRST, c2[0:1, :], jnp.where(
                        srow == STAT_CLAST, c2[tm - 1:tm, :], 0.0))))

    low = lane < HEAD_DIM
    for j in range(N_HEADS // 2):
        qb = zq[:, j * LANES:(j + 1) * LANES]
        if prompt:
            kb = zk[:, j * LANES:(j + 1) * LANES]
            vb = zv[:, j * LANES:(j + 1) * LANES]
        for par in range(2):
            hd = 2 * j + par
            if prompt:
                qs, ks, vs = (qb, kb, vb) if par == 0 else (
                    pltpu.roll(qb, HEAD_DIM, 1), pltpu.roll(kb, HEAD_DIM, 1), pltpu.roll(vb, HEAD_DIM, 1))
                qa_ref[hd] = jnp.where(low, qs, aug[:, hd * LANES:(hd + 1) * LANES]).astype(BF16)
                ka_ref[hd] = jnp.where(
                    low, ks, aug[:, (N_HEADS + hd) * LANES:(N_HEADS + hd + 1) * LANES]).astype(BF16)
                va_ref[hd] = jnp.where(low, vs, ones_v).astype(BF16)
            else:
                qs = qb if par == 0 else pltpu.roll(qb, HEAD_DIM, 1)
                qa_ref[hd] = jnp.where(low, qs, 0.0).astype(BF16)


def _proj(x, layer, g_pre_mix, w1, bf_pad, conv_w, g_conv_out, state, *, seq_len, prompt):
    n_tok, d = x.shape
    d_conv = conv_w.shape[-1]
    n_cols = w1.shape[-1]
    tm = min(TOKEN_TILE, n_tok)
    assert n_tok % tm == 0
    if seq_len >= tm:
        assert seq_len % tm == 0
        nseq, tiles_per_seq = 1, seq_len // tm
        st_map = lambda i: (i // tiles_per_seq, 0, 0)
    else:
        assert tm % seq_len == 0 and seq_len % SUBLANES == 0
        nseq, tiles_per_seq = tm // seq_len, 1
        st_map = lambda i: (i, 0, 0)
    seg = tm // nseq
    n_tiles = n_tok // tm
    row = lambda i: (i, 0)
    lay2 = lambda i: (layer, 0, 0)
    in_specs = [
        pl.BlockSpec((tm, d), row),
        pl.BlockSpec((None, 1, d), lay2),
        pl.BlockSpec((None, d, n_cols), lay2),
        pl.BlockSpec((None, 1, LANES), lay2),
        pl.BlockSpec((None, CONV_W, d_conv), lay2),
        pl.BlockSpec((None, 1, d_conv), lay2),
        pl.BlockSpec((nseq, CONV_W - 1, d_conv), st_map),
    ]
    args = [x, g_pre_mix, w1, bf_pad, conv_w, g_conv_out, state]
    head_spec = pl.BlockSpec((N_HEADS, tm, LANES), lambda i: (0, i, 0))
    out_specs = [
        pl.BlockSpec((tm, ATTN_DIM), row),
        pl.BlockSpec((tm, ATTN_DIM), row),
        pl.BlockSpec((tm, LANES), row),
        pl.BlockSpec((tm, d_conv), row),
        pl.BlockSpec((nseq, CONV_W - 1, d_conv), st_map),
        head_spec,
    ]
    out_shape = [
        jax.ShapeDtypeStruct((n_tok, ATTN_DIM), F32),
        jax.ShapeDtypeStruct((n_tok, ATTN_DIM), F32),
        jax.ShapeDtypeStruct((n_tok, LANES), F32),
        jax.ShapeDtypeStruct((n_tok, d_conv), BF16),
        jax.ShapeDtypeStruct(state.shape, F32),
        jax.ShapeDtypeStruct((N_HEADS, n_tok, LANES), BF16),
    ]
    scratch = [pltpu.VMEM((nseq * (seg + SUBLANES), d_conv), F32)]
    if prompt:
        tri = jnp.tril(jnp.ones((tm, tm), BF16))
        head_sel = jnp.asarray(
            np.arange(ATTN_DIM)[:, None] // HEAD_DIM == np.arange(LANES)[None, :], BF16)
        in_specs += [pl.BlockSpec((tm, tm), lambda i: (0, 0)),
                     pl.BlockSpec((LANES, 2 * N_HEADS * LANES), lambda i: (0, 0)),
                     pl.BlockSpec((ATTN_DIM, LANES), lambda i: (0, 0))]
        args += [tri, _placement_matrix(), head_sel]
        out_specs += [head_spec, head_spec, pl.BlockSpec((1, SUBLANES, LANES), lambda i: (i, 0, 0))]
        out_shape += [jax.ShapeDtypeStruct((N_HEADS, n_tok, LANES), BF16)] * 2
        out_shape += [jax.ShapeDtypeStruct((n_tiles, SUBLANES, LANES), F32)]
        scratch += [pltpu.VMEM((SUBLANES, LANES), F32)]
    kern = functools.partial(_proj_kernel, tm=tm, nseq=nseq, tiles_per_seq=tiles_per_seq, prompt=prompt)
    return pl.pallas_call(
        kern,
        grid=(n_tiles,),
        in_specs=in_specs,
        out_specs=out_specs,
        out_shape=out_shape,
        scratch_shapes=scratch,
        compiler_params=pltpu.CompilerParams(
            dimension_semantics=("arbitrary",), vmem_limit_bytes=VMEM_LIMIT_BYTES),
        name="proj_prompt" if prompt else "proj_sample",
    )(*args)


def _attn_prompt_kernel(stat_ref, q_ref, k_ref, v_ref, o_ref, p_scr, al_scr, m_scr, acc_scr, *, tq, tk, nbq):
    pair = pl.program_id(0)
    qi = pl.program_id(1) % nbq
    r = tq // tk
    tile0 = (pl.program_id(1) // nbq) * (nbq * r)
    d = (lax.broadcasted_iota(jnp.int32, (tq, tk), 1) - lax.broadcasted_iota(jnp.int32, (tq, tk), 0))
    lane = lax.broadcasted_iota(jnp.int32, (tq, LANES), 1)
    j_diag = qi * r
    outs = []
    for hh in range(2):
        hd = 2 * pair + hh

        def stat(tile, which, hd=hd):
            return stat_ref[((tile0 + tile) * N_STATS + which) * N_HEADS + hd]

        q_norm = functools.reduce(jnp.maximum, [stat(j_diag + a, STAT_QNORM) for a in range(r)])
        k_diag = functools.reduce(jnp.maximum, [stat(j_diag + a, STAT_KNORM) for a in range(r)])
        k_all = lax.fori_loop(0, nbq * r, lambda t, acc: jnp.maximum(acc, stat(t, STAT_KNORM)), jnp.float32(0.0))
        reach = q_norm * (k_all + k_diag) + stat(j_diag, STAT_CFIRST) + SKIP_LOG2
        j_first = lax.while_loop(
            lambda j: jnp.logical_and(j > 0, reach - stat(jnp.maximum(j - 1, 0), STAT_CLAST) >= 0.0),
            lambda j: j - 1, j_diag)

        def softmax_stage(j, diagonal, hh=hh):
            start = pl.multiple_of(j * tk, tk)
            s = _dot_nt(q_ref[hh], k_ref[hh, pl.ds(start, tk), :])
            if diagonal:
                s = jnp.where(d <= qi * tq - j * tk, s, NEG_INF)
            chunks = [s[:, c * LANES:(c + 1) * LANES] for c in range(tk // LANES)]
            smax = functools.reduce(jnp.maximum, chunks)
            m_old = m_scr[...]
            m_new = jnp.maximum(m_old, jnp.max(smax, axis=-1, keepdims=True))
            m_scr[...] = m_new
            p = jnp.concatenate([jnp.exp2(ch - m_new) for ch in chunks], axis=-1)
            return p.astype(BF16), jnp.exp2(m_old - m_new)

        def pv_stage(p, alpha, j, hh=hh):
            start = pl.multiple_of(j * tk, tk)
            acc_scr[...] = alpha * acc_scr[...] + _dot(p, v_ref[hh, pl.ds(start, tk), :])

        m_scr[...] = jnp.full(m_scr.shape, NEG_INF, F32)
        acc_scr[...] = jnp.zeros_like(acc_scr)
        p_scr[...], al_scr[...] = softmax_stage(j_diag + r - 1, True)

        def step(j, diagonal):
            p_prev, al_prev = p_scr[...], al_scr[...]
            p_scr[...], al_scr[...] = softmax_stage(j - 1, diagonal)
            pv_stage(p_prev, al_prev, j)

        for a in range(r - 1, 0, -1):
            step(j_diag + a, True)

        def body(t, carry):
            step(j_diag - t, False)
            return carry

        lax.fori_loop(0, j_diag - j_first, body, 0)
        pv_stage(p_scr[...], al_scr[...], j_first)
        acc = acc_scr[...]
        outs.append(acc / acc[:, ONES_LANE_V:ONES_LANE_V + 1])
    o_ref[...] = jnp.where(lane < HEAD_DIM, outs[0], pltpu.roll(outs[1], HEAD_DIM, 1))


def _attn_prompt(stats, qa, ka, va, *, seq_len):
    n_tok = qa.shape[1]
    tk = min(TOKEN_TILE, seq_len)
    tq = min(ATTN_QBLOCK, seq_len)
    assert seq_len % tq == 0 and tq % tk == 0 and n_tok % seq_len == 0
    n_seq = n_tok // seq_len
    nbq = seq_len // tq
    kv_spec = pl.BlockSpec((2, seq_len, LANES), lambda p, i: (p, i // nbq, 0))
    return pl.pallas_call(
        functools.partial(_attn_prompt_kernel, tq=tq, tk=tk, nbq=nbq),
        grid=(N_HEADS // 2, n_seq * nbq),
        in_specs=[pl.BlockSpec(memory_space=pltpu.SMEM),
                  pl.BlockSpec((2, tq, LANES), lambda p, i: (p, i, 0)), kv_spec, kv_spec],
        out_specs=pl.BlockSpec((tq, LANES), lambda p, i: (i, p)),
        out_shape=jax.ShapeDtypeStruct((n_tok, ATTN_DIM), F32),
        scratch_shapes=[pltpu.VMEM((tq, tk), BF16)] + [pltpu.VMEM((tq, LANES), F32)] * 3,
        compiler_params=pltpu.CompilerParams(
            dimension_semantics=("arbitrary", "arbitrary"), vmem_limit_bytes=VMEM_LIMIT_BYTES),
        name="attn_prompt",
    )(stats, qa, ka, va)


def _attn_sample_kernel(qz_ref, kn_ref, vn_ref, kc_ref, vc_ref, lft_ref, lf_ref, u_ref, o_ref, *, t_new, p_len):
    n_ext = lft_ref.shape[-1]
    n_chunks = n_ext // CUMSUM_CHUNK
    u = u_ref[...]

    hi, mid, lo = _split3(lft_ref[...])
    x3 = jnp.concatenate([hi, mid, lo, jnp.zeros_like(hi)], axis=0).astype(BF16)
    offs = jnp.zeros((N_HEADS, 1), F32)
    pieces = []
    for c in range(n_chunks):
        loc3 = _dot(x3[:, c * CUMSUM_CHUNK:(c + 1) * CUMSUM_CHUNK], u)
        loc = loc3[0:N_HEADS] + loc3[N_HEADS:2 * N_HEADS] + loc3[2 * N_HEADS:3 * N_HEADS]
        pieces.append(loc + offs)
        offs = offs + loc[:, CUMSUM_CHUNK - 1:CUMSUM_CHUNK]
    c_abs = jnp.concatenate(pieces, axis=1)
    p_tot = c_abs[:, p_len - 1:p_len]
    r2 = (p_tot - c_abs) * LOG2E

    rown = lax.broadcasted_iota(jnp.int32, (t_new, t_new), 0)
    coln = lax.broadcasted_iota(jnp.int32, (t_new, t_new), 1)
    tri = (coln <= rown).astype(BF16)
    nhi, nmid, nlo = _split3(lf_ref[...])
    cn2 = (_dot(tri, nhi.astype(BF16)) + _dot(tri, nmid.astype(BF16)) + _dot(tri, nlo.astype(BF16))) * LOG2E

    kn = kn_ref[...].astype(BF16)
    vn = vn_ref[...].astype(BF16)
    for hd in range(N_HEADS):
        sl = slice(hd * HEAD_DIM, (hd + 1) * HEAD_DIM)
        q = qz_ref[hd][:, :HEAD_DIM]
        cq = cn2[:, hd:hd + 1]
        sp = _dot(q, kc_ref[hd].astype(BF16)) + cq + r2[hd:hd + 1, 0:p_len]
        sn = _dot_nt(q, kn[:, sl]) + cq + r2[hd:hd + 1, p_len:p_len + t_new]
        sn = jnp.where(coln <= rown, sn, NEG_INF)
        m = jnp.maximum(jnp.max(sp, axis=-1, keepdims=True), jnp.max(sn, axis=-1, keepdims=True))
        pp = jnp.exp2(sp - m)
        pn = jnp.exp2(sn - m)
        l = jnp.sum(pp, axis=-1, keepdims=True) + jnp.sum(pn, axis=-1, keepdims=True)
        o_ref[:, sl] = (_dot_nt(pp.astype(BF16), vc_ref[hd].astype(BF16)) + _dot(pn.astype(BF16), vn[:, sl])) / l


def _attn_sample(qz, k_new, v_new, cache_kt, cache_vt, lft_ext, lf_pad, layer, *, t_new):
    n_tok = k_new.shape[0]
    n_b = n_tok // t_new
    p_len = cache_kt.shape[-1]
    n_ext = lft_ext.shape[-1]
    assert p_len % CUMSUM_CHUNK == 0 and n_ext % CUMSUM_CHUNK == 0 and t_new % SUBLANES == 0
    u = jnp.triu(jnp.ones((CUMSUM_CHUNK, CUMSUM_CHUNK), BF16))
    tok = lambda b: (b, 0)
    cache_spec = pl.BlockSpec((None, None, N_HEADS, HEAD_DIM, p_len), lambda b: (layer, b, 0, 0, 0))
    return pl.pallas_call(
        functools.partial(_attn_sample_kernel, t_new=t_new, p_len=p_len),
        grid=(n_b,),
        in_specs=[
            pl.BlockSpec((N_HEADS, t_new, LANES), lambda b: (0, b, 0)),
            pl.BlockSpec((t_new, ATTN_DIM), tok),
            pl.BlockSpec((t_new, ATTN_DIM), tok),
            cache_spec, cache_spec,
            pl.BlockSpec((None, N_HEADS, n_ext), lambda b: (b, 0, 0)),
            pl.BlockSpec((t_new, LANES), tok),
            pl.BlockSpec((CUMSUM_CHUNK, CUMSUM_CHUNK), lambda b: (0, 0)),
        ],
        out_specs=pl.BlockSpec((t_new, ATTN_DIM), tok),
        out_shape=jax.ShapeDtypeStruct((n_tok, ATTN_DIM), F32),
        compiler_params=pltpu.CompilerParams(
            dimension_semantics=("arbitrary",), vmem_limit_bytes=VMEM_LIMIT_BYTES),
        name="attn_sample",
    )(qz, k_new, v_new, cache_kt, cache_vt, lft_ext, lf_pad, u)


def _post_kernel(x_ref, oa_ref, ocn_ref, gao_ref, wo_ref, gpm_ref, gpl_ref, wup_ref, wdn_ref, gpo_ref, y_ref):
    d_ff = wup_ref.shape[1]
    oan = _rms(oa_ref[...], gao_ref[...]).astype(BF16)
    merged = jnp.concatenate([oan, ocn_ref[...]], axis=-1)
    x1 = x_ref[...] + _rms(_dot(merged, wo_ref[...]), gpm_ref[...])
    h2 = _rms(x1, gpl_ref[...]).astype(BF16)
    m = None
    for c in range(d_ff // FF_CHUNK):
        sl = slice(c * FF_CHUNK, (c + 1) * FF_CHUNK)
        a = jnp.square(jnp.maximum(_dot(h2, wup_ref[:, sl]), 0.0)).astype(BF16)
        part = _dot(a, wdn_ref[sl, :])
        m = part if m is None else m + part
    y_ref[...] = x1 + _rms(m, gpo_ref[...])


def _post(x, oa, ocn, layer, g_attn_out, w_o, g_post_mix, g_pre_mlp, w_up, w_down, g_post_mlp):
    n_tok, d = x.shape
    d_ff = w_up.shape[-1]
    tm = min(TOKEN_TILE, n_tok)
    assert n_tok % tm == 0 and d_ff % FF_CHUNK == 0
    row = lambda i: (i, 0)
    lay = lambda i: (layer, 0, 0)
    vec = lambda n: pl.BlockSpec((None, 1, n), lay)
    return pl.pallas_call(
        _post_kernel,
        grid=(n_tok // tm,),
        in_specs=[
            pl.BlockSpec((tm, d), row),
            pl.BlockSpec((tm, oa.shape[1]), row),
            pl.BlockSpec((tm, ocn.shape[1]), row),
            vec(oa.shape[1]),
            pl.BlockSpec((None, d, d), lay),
            vec(d), vec(d),
            pl.BlockSpec((None, d, d_ff), lay),
            pl.BlockSpec((None, d_ff, d), lay),
            vec(d),
        ],
        out_specs=pl.BlockSpec((tm, d), row),
        out_shape=jax.ShapeDtypeStruct((n_tok, d), F32),
        compiler_params=pltpu.CompilerParams(
            dimension_semantics=("arbitrary",), vmem_limit_bytes=VMEM_LIMIT_BYTES),
        name="post",
    )(x, oa, ocn, g_attn_out, w_o, g_post_mix, g_pre_mlp, w_up, w_down, g_post_mlp)


def kernel(x_prompt, x_sample, cache_k, cache_v, cache_logf, state_conv, g_pre_mix, w_in, b_f, conv_w,
           g_attn_out, g_conv_out, w_o, g_post_mix, g_pre_mlp, w_up, w_down, g_post_mlp):
    depth = w_in.shape[0]
    batch, seq, d = x_prompt.shape
    dec_batch, dec_seq, _ = x_sample.shape
    past_len = cache_k.shape[2]
    d_conv = conv_w.shape[-1]
    assert cache_k.shape[3:] == (N_HEADS, HEAD_DIM) and d_conv + ATTN_DIM == d

    c0, c1 = 3 * ATTN_DIM, 3 * ATTN_DIM + N_HEADS
    w1 = jnp.concatenate(
        [w_in[:, :, :c0], w_in[:, :, c1:], w_in[:, :, c0:c1],
         jnp.zeros((depth, d, LANES - N_HEADS), w_in.dtype)], axis=-1).astype(BF16)
    bf_pad = jnp.pad(b_f, ((0, 0), (0, LANES - N_HEADS)))[:, None, :]
    w_o_b, w_up_b, w_dn_b = w_o.astype(BF16), w_up.astype(BF16), w_down.astype(BF16)
    vec3 = lambda a: a[:, None, :]
    g_pre_mix3, g_conv_out3, g_attn_out3 = vec3(g_pre_mix), vec3(g_conv_out), vec3(g_attn_out)
    g_post_mix3, g_pre_mlp3, g_post_mlp3 = vec3(g_post_mix), vec3(g_pre_mlp), vec3(g_post_mlp)

    ck = jnp.transpose(cache_k, (0, 1, 3, 4, 2))
    cv = jnp.transpose(cache_v, (0, 1, 3, 4, 2))
    past_lft = jnp.swapaxes(cache_logf, 2, 3)
    n_ext = past_len + CUMSUM_CHUNK * pl.cdiv(dec_seq, CUMSUM_CHUNK)
    zero_state = jnp.zeros((batch, CONV_W - 1, d_conv), F32)

    xp = x_prompt.reshape(batch * seq, d)
    xs = x_sample.reshape(dec_batch * dec_seq, d)
    outs = [[] for _ in range(8)]
    for l in range(depth):
        kp, vp, lfp, ocn_p, cst_p, qa, ka, va, stats = _proj(
            xp, l, g_pre_mix3, w1, bf_pad, conv_w, g_conv_out3, zero_state, seq_len=seq, prompt=True)
        ks, vs, lfs, ocn_s, cst_s, qz = _proj(
            xs, l, g_pre_mix3, w1, bf_pad, conv_w, g_conv_out3, state_conv[l], seq_len=dec_seq, prompt=False)
        oa_p = _attn_prompt(stats[:, :N_STATS, :N_HEADS].reshape(-1), qa, ka, va, seq_len=seq)
        new_lft = jnp.swapaxes(lfs[:, :N_HEADS].reshape(dec_batch, dec_seq, N_HEADS), 1, 2)
        lft_ext = jnp.concatenate(
            [past_lft[l], new_lft, jnp.zeros((dec_batch, N_HEADS, n_ext - past_len - dec_seq), F32)], axis=-1)
        oa_s = _attn_sample(qz, ks, vs, ck, cv, lft_ext, lfs, l, t_new=dec_seq)
        xp = _post(xp, oa_p, ocn_p, l, g_attn_out3, w_o_b, g_post_mix3, g_pre_mlp3, w_up_b, w_dn_b, g_post_mlp3)
        xs = _post(xs, oa_s, ocn_s, l, g_attn_out3, w_o_b, g_post_mix3, g_pre_mlp3, w_up_b, w_dn_b, g_post_mlp3)
        for lst, val in zip(outs, (
                kp.reshape(batch, seq, N_HEADS, HEAD_DIM), vp.reshape(batch, seq, N_HEADS, HEAD_DIM),
                lfp[:, :N_HEADS].reshape(batch, seq, N_HEADS), cst_p,
                ks.reshape(dec_batch, dec_seq, N_HEADS, HEAD_DIM), vs.reshape(dec_batch, dec_seq, N_HEADS, HEAD_DIM),
                lfs[:, :N_HEADS].reshape(dec_batch, dec_seq, N_HEADS), cst_s)):
            lst.append(val)
    return (xp.reshape(batch, seq, d), xs.reshape(dec_batch, dec_seq, d)) + tuple(jnp.stack(o) for o in outs)
```

```python
import functools
import math

import numpy as np
import jax
import jax.numpy as jnp
from jax import lax
from jax.experimental import pallas as pl
from jax.experimental.pallas import tpu as pltpu

N_HEADS = 8
HEAD_DIM = 64
ATTN_DIM = N_HEADS * HEAD_DIM
CONV_W = 3
EPS = 1e-6
NEG_INF = -1e30
LOG2E = math.log2(math.e)

LANES = 128
SUBLANES = 8
VMEM_LIMIT_BYTES = 56 * 1024 * 1024

TOKEN_TILE = 512
ATTN_QBLOCK = 1024
CUMSUM_CHUNK = 512
FF_CHUNK = 1024

F32 = jnp.float32
BF16 = jnp.bfloat16

AUG_Q0 = HEAD_DIM
AUG_K0 = HEAD_DIM
ONES_LANE_V = HEAD_DIM
CP_ONE_LANE = 3 * N_HEADS

STAT_QNORM, STAT_KNORM, STAT_OWNMIN, STAT_CFIRST, STAT_CLAST, N_STATS = 0, 1, 2, 3, 4, 5
NORM_MARGIN = 1.0 + 2.0 ** -6
SKIP_LOG2 = 160.0


def _split3(x):
    hi = x.astype(BF16).astype(F32)
    r = x - hi
    mid = r.astype(BF16).astype(F32)
    lo = (r - mid).astype(BF16).astype(F32)
    return hi, mid, lo


def _rms(x, g):
    return x * lax.rsqrt(jnp.mean(x * x, axis=-1, keepdims=True) + EPS) * g


def _log_sigmoid(x):
    return jnp.minimum(x, 0.0) - jnp.log1p(jnp.exp(-jnp.abs(x)))


def _dot(a, b):
    return jnp.dot(a, b, preferred_element_type=F32)


def _dot_nt(a, b):
    return lax.dot_general(a, b, (((1,), (1,)), ((), ())), preferred_element_type=F32)


def _placement_matrix():
    p = np.zeros((LANES, 2 * N_HEADS * LANES), np.float32)
    for h in range(N_HEADS):
        for j in range(3):
            p[j * N_HEADS + h, h * LANES + AUG_Q0 + j] = 1.0
            p[CP_ONE_LANE, h * LANES + AUG_Q0 + 3 + j] = 1.0
            p[CP_ONE_LANE, N_HEADS * LANES + h * LANES + AUG_K0 + j] = 1.0
            p[j * N_HEADS + h, N_HEADS * LANES + h * LANES + AUG_K0 + 3 + j] = -1.0
    return jnp.asarray(p, BF16)


PREP_ROWS = 256


def _prep_w_in_kernel(w_ref, o_ref):
    c0 = 3 * ATTN_DIM
    n_rest = w_ref.shape[1] - c0 - N_HEADS
    o_ref[:, 0:c0] = w_ref[:, 0:c0].astype(BF16)
    tail = w_ref[:, c0:]
    o_ref[:, c0:c0 + n_rest] = tail[:, N_HEADS:].astype(BF16)
    lane = lax.broadcasted_iota(jnp.int32, (w_ref.shape[0], LANES), 1)
    o_ref[:, c0 + n_rest:] = jnp.where(lane < N_HEADS, tail[:, 0:LANES], 0.0).astype(BF16)


def _prep_w_in(w_in):
    depth, d, n_in = w_in.shape
    n_out = n_in - N_HEADS + LANES
    assert d % PREP_ROWS == 0 and (n_in - N_HEADS) % LANES == 0
    return pl.pallas_call(
        _prep_w_in_kernel,
        grid=(depth, d // PREP_ROWS),
        in_specs=[pl.BlockSpec((None, PREP_ROWS, n_in), lambda l, i: (l, i, 0))],
        out_specs=pl.BlockSpec((None, PREP_ROWS, n_out), lambda l, i: (l, i, 0)),
        out_shape=jax.ShapeDtypeStruct((depth, d, n_out), BF16),
        compiler_params=pltpu.CompilerParams(
            dimension_semantics=("arbitrary", "arbitrary"), vmem_limit_bytes=VMEM_LIMIT_BYTES),
        name="prep_w_in",
    )(w_in)


def _proj_kernel(*refs, tm, nseq, tiles_per_seq, prompt):
    if prompt:
        (x_ref, g_ref, w_ref, bf_ref, cw_ref, gco_ref, st_ref, tri_ref, pm_ref, sel_ref,
         k_ref, v_ref, lf_ref, ocn_ref, nst_ref, qa_ref, ka_ref, va_ref, stat_ref,
         cu_scr, cc_scr) = refs
    else:
        (x_ref, g_ref, w_ref, bf_ref, cw_ref, gco_ref, st_ref,
         k_ref, v_ref, lf_ref, ocn_ref, nst_ref, qa_ref,
         cu_scr) = refs
    i = pl.program_id(0)
    d_attn = ATTN_DIM
    d_conv = cw_ref.shape[1]

    h = _rms(x_ref[...], g_ref[...]).astype(BF16)
    z = _dot(h, w_ref[...])
    zq = z[:, 0:d_attn] * (HEAD_DIM ** -0.5 * LOG2E)
    zk = z[:, d_attn:2 * d_attn]
    zv = z[:, 2 * d_attn:3 * d_attn]
    o = 3 * d_attn
    gate_b = z[:, o:o + d_conv]
    cu = z[:, o + d_conv:o + 2 * d_conv] * z[:, o + 2 * d_conv:o + 3 * d_conv]
    fz = z[:, o + 3 * d_conv:o + 3 * d_conv + LANES] + bf_ref[...]

    k_ref[...] = zk
    v_ref[...] = zv
    lane = lax.broadcasted_iota(jnp.int32, (tm, LANES), 1)
    logf = jnp.where(lane < N_HEADS, _log_sigmoid(fz), 0.0)
    lf_ref[...] = logf

    seg = tm // nseq
    slot = seg + SUBLANES
    if nseq == 1:
        first = (i % tiles_per_seq) == 0

        @pl.when(first)
        def _():
            cu_scr[6:8, :] = st_ref[0]

        @pl.when(jnp.logical_not(first))
        def _():
            cu_scr[6:8, :] = cu_scr[seg + 6:seg + 8, :]

        cu_scr[8:8 + seg, :] = cu
        cu_m1 = cu_scr[7:7 + seg, :]
        cu_m2 = cu_scr[6:6 + seg, :]
        nst_ref[0] = cu_scr[seg + 6:seg + 8, :]
    else:
        for s in range(nseq):
            cu_scr[s * slot + 6:s * slot + 8, :] = st_ref[s]
            cu_scr[s * slot + 8:s * slot + 8 + seg, :] = cu[s * seg:(s + 1) * seg, :]
        cu_m1 = jnp.concatenate([cu_scr[s * slot + 7:s * slot + 7 + seg, :] for s in range(nseq)], axis=0)
        cu_m2 = jnp.concatenate([cu_scr[s * slot + 6:s * slot + 6 + seg, :] for s in range(nseq)], axis=0)
        for s in range(nseq):
            nst_ref[s] = cu_scr[s * slot + seg + 6:s * slot + seg + 8, :]
    cw = cw_ref[...]
    y = cw[0:1, :] * cu_m2 + cw[1:2, :] * cu_m1 + cw[2:3, :] * cu
    ocn_ref[...] = _rms(gate_b * y, gco_ref[...]).astype(BF16)

    if prompt:
        tri = tri_ref[...]
        hi, mid, lo = _split3(logf)
        loc = _dot(tri, hi.astype(BF16)) + _dot(tri, mid.astype(BF16)) + _dot(tri, lo.astype(BF16))

        @pl.when((i % tiles_per_seq) == 0)
        def _():
            cc_scr[...] = jnp.zeros_like(cc_scr)

        c = loc + cc_scr[SUBLANES - 1:SUBLANES, :]
        cc_scr[...] = c[tm - SUBLANES:tm, :]
        c2 = c * LOG2E
        chi, cmid, clo = _split3(c2)
        cp = (chi + pltpu.roll(cmid, N_HEADS, 1) + pltpu.roll(clo, 2 * N_HEADS, 1)
              + (lane == CP_ONE_LANE).astype(F32))
        aug = _dot(cp.astype(BF16), pm_ref[...])
        ones_v = (lane == ONES_LANE_V).astype(F32)

        zqr = zq.astype(BF16).astype(F32)
        zkr = zk.astype(BF16).astype(F32)
        head_sum = lambda prod: _dot(prod.astype(BF16), sel_ref[...])
        qn = jnp.sqrt(jnp.max(head_sum(zqr * zqr), axis=0, keepdims=True)) * NORM_MARGIN
        kn = jnp.sqrt(jnp.max(head_sum(zkr * zkr), axis=0, keepdims=True)) * NORM_MARGIN
        own = jnp.min(head_sum(zqr * zkr), axis=0, keepdims=True) - (NORM_MARGIN - 1.0) * qn * kn
        srow = lax.broadcasted_iota(jnp.int32, (SUBLANES, LANES), 0)
        stat_ref[0] = jnp.where(
            srow == STAT_QNORM, qn, jnp.where(
                srow == STAT_KNORM, kn, jnp.where(
                    srow == STAT_OWNMIN, own, jnp.where(
                        srow == STAT_CFIRST, c2[0:1, :], jnp.where(
                            srow == STAT_CLAST, c2[tm - 1:tm, :], 0.0)))))

    low = lane < HEAD_DIM
    for j in range(N_HEADS // 2):
        qb = zq[:, j * LANES:(j + 1) * LANES]
        if prompt:
            kb = zk[:, j * LANES:(j + 1) * LANES]
            vb = zv[:, j * LANES:(j + 1) * LANES]
        for par in range(2):
            hd = 2 * j + par
            if prompt:
                qs, ks, vs = (qb, kb, vb) if par == 0 else (
                    pltpu.roll(qb, HEAD_DIM, 1), pltpu.roll(kb, HEAD_DIM, 1), pltpu.roll(vb, HEAD_DIM, 1))
                qa_ref[hd] = jnp.where(low, qs, aug[:, hd * LANES:(hd + 1) * LANES]).astype(BF16)
                ka_ref[hd] = jnp.where(
                    low, ks, aug[:, (N_HEADS + hd) * LANES:(N_HEADS + hd + 1) * LANES]).astype(BF16)
                va_ref[hd] = jnp.where(low, vs, ones_v).astype(BF16)
            else:
                qs = qb if par == 0 else pltpu.roll(qb, HEAD_DIM, 1)
                qa_ref[hd] = jnp.where(low, qs, 0.0).astype(BF16)


def _proj(x, layer, g_pre_mix, w1, bf_pad, conv_w, g_conv_out, state, *, seq_len, prompt):
    n_tok, d = x.shape
    d_conv = conv_w.shape[-1]
    n_cols = w1.shape[-1]
    tm = min(TOKEN_TILE, n_tok)
    assert n_tok % tm == 0
    if seq_len >= tm:
        assert seq_len % tm == 0
        nseq, tiles_per_seq = 1, seq_len // tm
        st_map = lambda i: (i // tiles_per_seq, 0, 0)
    else:
        assert tm % seq_len == 0 and seq_len % SUBLANES == 0
        nseq, tiles_per_seq = tm // seq_len, 1
        st_map = lambda i: (i, 0, 0)
    seg = tm // nseq
    n_tiles = n_tok // tm
    row = lambda i: (i, 0)
    lay2 = lambda i: (layer, 0, 0)
    in_specs = [
        pl.BlockSpec((tm, d), row),
        pl.BlockSpec((None, 1, d), lay2),
        pl.BlockSpec((None, d, n_cols), lay2),
        pl.BlockSpec((None, 1, LANES), lay2),
        pl.BlockSpec((None, CONV_W, d_conv), lay2),
        pl.BlockSpec((None, 1, d_conv), lay2),
        pl.BlockSpec((nseq, CONV_W - 1, d_conv), st_map),
    ]
    args = [x, g_pre_mix, w1, bf_pad, conv_w, g_conv_out, state]
    head_spec = pl.BlockSpec((N_HEADS, tm, LANES), lambda i: (0, i, 0))
    out_specs = [
        pl.BlockSpec((tm, ATTN_DIM), row),
        pl.BlockSpec((tm, ATTN_DIM), row),
        pl.BlockSpec((tm, LANES), row),
        pl.BlockSpec((tm, d_conv), row),
        pl.BlockSpec((nseq, CONV_W - 1, d_conv), st_map),
        head_spec,
    ]
    out_shape = [
        jax.ShapeDtypeStruct((n_tok, ATTN_DIM), F32),
        jax.ShapeDtypeStruct((n_tok, ATTN_DIM), F32),
        jax.ShapeDtypeStruct((n_tok, LANES), F32),
        jax.ShapeDtypeStruct((n_tok, d_conv), BF16),
        jax.ShapeDtypeStruct(state.shape, F32),
        jax.ShapeDtypeStruct((N_HEADS, n_tok, LANES), BF16),
    ]
    scratch = [pltpu.VMEM((nseq * (seg + SUBLANES), d_conv), F32)]
    if prompt:
        tri = jnp.tril(jnp.ones((tm, tm), BF16))
        head_sel = jnp.asarray(
            np.arange(ATTN_DIM)[:, None] // HEAD_DIM == np.arange(LANES)[None, :], BF16)
        in_specs += [pl.BlockSpec((tm, tm), lambda i: (0, 0)),
                     pl.BlockSpec((LANES, 2 * N_HEADS * LANES), lambda i: (0, 0)),
                     pl.BlockSpec((ATTN_DIM, LANES), lambda i: (0, 0))]
        args += [tri, _placement_matrix(), head_sel]
        out_specs += [head_spec, head_spec, pl.BlockSpec((1, SUBLANES, LANES), lambda i: (i, 0, 0))]
        out_shape += [jax.ShapeDtypeStruct((N_HEADS, n_tok, LANES), BF16)] * 2
        out_shape += [jax.ShapeDtypeStruct((n_tiles, SUBLANES, LANES), F32)]
        scratch += [pltpu.VMEM((SUBLANES, LANES), F32)]
    kern = functools.partial(_proj_kernel, tm=tm, nseq=nseq, tiles_per_seq=tiles_per_seq, prompt=prompt)
    return pl.pallas_call(
        kern,
        grid=(n_tiles,),
        in_specs=in_specs,
        out_specs=out_specs,
        out_shape=out_shape,
        scratch_shapes=scratch,
        compiler_params=pltpu.CompilerParams(
            dimension_semantics=("arbitrary",), vmem_limit_bytes=VMEM_LIMIT_BYTES),
        name="proj_prompt" if prompt else "proj_sample",
    )(*args)


def _attn_prompt_kernel(stat_ref, q_ref, k_ref, v_ref, o_ref, p_scr, al_scr, m_scr, acc_scr, *, tq, tk, nbq):
    pair = pl.program_id(0)
    qi = pl.program_id(1) % nbq
    r = tq // tk
    tile0 = (pl.program_id(1) // nbq) * (nbq * r)
    d = (lax.broadcasted_iota(jnp.int32, (tq, tk), 1) - lax.broadcasted_iota(jnp.int32, (tq, tk), 0))
    lane = lax.broadcasted_iota(jnp.int32, (tq, LANES), 1)
    j_diag = qi * r

    def first_block(hd):
        def stat(tile, which):
            return stat_ref[((tile0 + tile) * N_STATS + which) * N_HEADS + hd]

        q_norm = functools.reduce(jnp.maximum, [stat(j_diag + a, STAT_QNORM) for a in range(r)])
        own_min = functools.reduce(jnp.minimum, [stat(j_diag + a, STAT_OWNMIN) for a in range(r)])
        k_all = lax.fori_loop(0, nbq * r, lambda t, acc: jnp.maximum(acc, stat(t, STAT_KNORM)), jnp.float32(0.0))
        reach = q_norm * k_all - own_min + stat(j_diag, STAT_CFIRST) + SKIP_LOG2
        return lax.while_loop(
            lambda j: jnp.logical_and(j > 0, reach - stat(jnp.maximum(j - 1, 0), STAT_CLAST) >= 0.0),
            lambda j: j - 1, j_diag)

    j_first = [first_block(2 * pair + hh) for hh in range(2)]

    def softmax_stage(hh, j, row0, diagonal):
        start = pl.multiple_of(j * tk, tk)
        s = _dot_nt(q_ref[hh, row0:, :], k_ref[hh, pl.ds(start, tk), :])
        if diagonal:
            s = jnp.where(d[row0:, :] <= qi * tq - j * tk, s, NEG_INF)
        chunks = [s[:, c * LANES:(c + 1) * LANES] for c in range(tk // LANES)]
        smax = functools.reduce(jnp.maximum, chunks)
        m_old = m_scr[hh, row0:, :]
        m_new = jnp.maximum(m_old, jnp.max(smax, axis=-1, keepdims=True))
        m_scr[hh, row0:, :] = m_new
        p = jnp.concatenate([jnp.exp2(ch - m_new) for ch in chunks], axis=-1)
        p_scr[hh, row0:, :] = p.astype(BF16)
        al_scr[hh, row0:, :] = jnp.exp2(m_old - m_new)

    def pv_stage(hh, p, alpha, j, row0):
        start = pl.multiple_of(j * tk, tk)
        acc_scr[hh, row0:, :] = alpha * acc_scr[hh, row0:, :] + _dot(p, v_ref[hh, pl.ds(start, tk), :])

    def step(hh, j, row0_prev, row0, diagonal):
        p_prev, al_prev = p_scr[hh, row0_prev:, :], al_scr[hh, row0_prev:, :]
        softmax_stage(hh, j - 1, row0, diagonal)
        pv_stage(hh, p_prev, al_prev, j, row0_prev)

    m_scr[...] = jnp.full(m_scr.shape, NEG_INF, F32)
    acc_scr[...] = jnp.zeros_like(acc_scr)
    for hh in range(2):
        softmax_stage(hh, j_diag + r - 1, (r - 1) * tk, True)
    for a in range(r - 1, 0, -1):
        for hh in range(2):
            step(hh, j_diag + a, a * tk, (a - 1) * tk, True)
    for hh in range(2):
        def body(t, carry, hh=hh):
            step(hh, j_diag - t, 0, 0, False)
            return carry

        lax.fori_loop(0, j_diag - j_first[hh], body, 0)
    outs = []
    for hh in range(2):
        pv_stage(hh, p_scr[hh], al_scr[hh], j_first[hh], 0)
        acc = acc_scr[hh]
        outs.append(acc / acc[:, ONES_LANE_V:ONES_LANE_V + 1])
    o_ref[...] = jnp.where(lane < HEAD_DIM, outs[0], pltpu.roll(outs[1], HEAD_DIM, 1))


def _attn_prompt(stats, qa, ka, va, *, seq_len):
    n_tok = qa.shape[1]
    tk = min(TOKEN_TILE, seq_len)
    tq = min(ATTN_QBLOCK, seq_len)
    assert seq_len % tq == 0 and tq % tk == 0 and n_tok % seq_len == 0
    n_seq = n_tok // seq_len
    nbq = seq_len // tq
    kv_spec = pl.BlockSpec((2, seq_len, LANES), lambda p, i: (p, i // nbq, 0))
    return pl.pallas_call(
        functools.partial(_attn_prompt_kernel, tq=tq, tk=tk, nbq=nbq),
        grid=(N_HEADS // 2, n_seq * nbq),
        in_specs=[pl.BlockSpec(memory_space=pltpu.SMEM),
                  pl.BlockSpec((2, tq, LANES), lambda p, i: (p, i, 0)), kv_spec, kv_spec],
        out_specs=pl.BlockSpec((tq, LANES), lambda p, i: (i, p)),
        out_shape=jax.ShapeDtypeStruct((n_tok, ATTN_DIM), F32),
        scratch_shapes=[pltpu.VMEM((2, tq, tk), BF16)] + [pltpu.VMEM((2, tq, LANES), F32)] * 3,
        compiler_params=pltpu.CompilerParams(
            dimension_semantics=("arbitrary", "arbitrary"), vmem_limit_bytes=VMEM_LIMIT_BYTES),
        name="attn_prompt",
    )(stats, qa, ka, va)


def _attn_sample_kernel(qz_ref, kn_ref, vn_ref, kc_ref, vc_ref, lft_ref, lf_ref, u_ref, o_ref, *, t_new, p_len):
    n_ext = lft_ref.shape[-1]
    n_chunks = n_ext // CUMSUM_CHUNK
    u = u_ref[...]

    hi, mid, lo = _split3(lft_ref[...])
    x3 = jnp.concatenate([hi, mid, lo, jnp.zeros_like(hi)], axis=0).astype(BF16)
    offs = jnp.zeros((N_HEADS, 1), F32)
    pieces = []
    for c in range(n_chunks):
        loc3 = _dot(x3[:, c * CUMSUM_CHUNK:(c + 1) * CUMSUM_CHUNK], u)
        loc = loc3[0:N_HEADS] + loc3[N_HEADS:2 * N_HEADS] + loc3[2 * N_HEADS:3 * N_HEADS]
        pieces.append(loc + offs)
        offs = offs + loc[:, CUMSUM_CHUNK - 1:CUMSUM_CHUNK]
    c_abs = jnp.concatenate(pieces, axis=1)
    p_tot = c_abs[:, p_len - 1:p_len]
    r2 = (p_tot - c_abs) * LOG2E

    rown = lax.broadcasted_iota(jnp.int32, (t_new, t_new), 0)
    coln = lax.broadcasted_iota(jnp.int32, (t_new, t_new), 1)
    tri = (coln <= rown).astype(BF16)
    nhi, nmid, nlo = _split3(lf_ref[...])
    cn2 = (_dot(tri, nhi.astype(BF16)) + _dot(tri, nmid.astype(BF16)) + _dot(tri, nlo.astype(BF16))) * LOG2E

    kn = kn_ref[...].astype(BF16)
    vn = vn_ref[...].astype(BF16)
    for hd in range(N_HEADS):
        sl = slice(hd * HEAD_DIM, (hd + 1) * HEAD_DIM)
        q = qz_ref[hd][:, :HEAD_DIM]
        cq = cn2[:, hd:hd + 1]
        sp = _dot(q, kc_ref[hd].astype(BF16)) + cq + r2[hd:hd + 1, 0:p_len]
        sn = _dot_nt(q, kn[:, sl]) + cq + r2[hd:hd + 1, p_len:p_len + t_new]
        sn = jnp.where(coln <= rown, sn, NEG_INF)
        m = jnp.maximum(jnp.max(sp, axis=-1, keepdims=True), jnp.max(sn, axis=-1, keepdims=True))
        pp = jnp.exp2(sp - m)
        pn = jnp.exp2(sn - m)
        l = jnp.sum(pp, axis=-1, keepdims=True) + jnp.sum(pn, axis=-1, keepdims=True)
        o_ref[:, sl] = (_dot_nt(pp.astype(BF16), vc_ref[hd].astype(BF16)) + _dot(pn.astype(BF16), vn[:, sl])) / l


def _attn_sample(qz, k_new, v_new, cache_kt, cache_vt, lft_ext, lf_pad, layer, *, t_new):
    n_tok = k_new.shape[0]
    n_b = n_tok // t_new
    p_len = cache_kt.shape[-1]
    n_ext = lft_ext.shape[-1]
    assert p_len % CUMSUM_CHUNK == 0 and n_ext % CUMSUM_CHUNK == 0 and t_new % SUBLANES == 0
    u = jnp.triu(jnp.ones((CUMSUM_CHUNK, CUMSUM_CHUNK), BF16))
    tok = lambda b: (b, 0)
    cache_spec = pl.BlockSpec((None, None, N_HEADS, HEAD_DIM, p_len), lambda b: (layer, b, 0, 0, 0))
    return pl.pallas_call(
        functools.partial(_attn_sample_kernel, t_new=t_new, p_len=p_len),
        grid=(n_b,),
        in_specs=[
            pl.BlockSpec((N_HEADS, t_new, LANES), lambda b: (0, b, 0)),
            pl.BlockSpec((t_new, ATTN_DIM), tok),
            pl.BlockSpec((t_new, ATTN_DIM), tok),
            cache_spec, cache_spec,
            pl.BlockSpec((None, N_HEADS, n_ext), lambda b: (b, 0, 0)),
            pl.BlockSpec((t_new, LANES), tok),
            pl.BlockSpec((CUMSUM_CHUNK, CUMSUM_CHUNK), lambda b: (0, 0)),
        ],
        out_specs=pl.BlockSpec((t_new, ATTN_DIM), tok),
        out_shape=jax.ShapeDtypeStruct((n_tok, ATTN_DIM), F32),
        compiler_params=pltpu.CompilerParams(
            dimension_semantics=("arbitrary",), vmem_limit_bytes=VMEM_LIMIT_BYTES),
        name="attn_sample",
    )(qz, k_new, v_new, cache_kt, cache_vt, lft_ext, lf_pad, u)


def _post_kernel(x_ref, oa_ref, ocn_ref, gao_ref, wo_ref, gpm_ref, gpl_ref, wup_ref, wdn_ref, gpo_ref, y_ref):
    d_ff = wup_ref.shape[1]
    oan = _rms(oa_ref[...], gao_ref[...]).astype(BF16)
    merged = jnp.concatenate([oan, ocn_ref[...]], axis=-1)
    x1 = x_ref[...] + _rms(_dot(merged, wo_ref[...]), gpm_ref[...])
    h2 = _rms(x1, gpl_ref[...]).astype(BF16)
    m = None
    for c in range(d_ff // FF_CHUNK):
        sl = slice(c * FF_CHUNK, (c + 1) * FF_CHUNK)
        a = jnp.square(jnp.maximum(_dot(h2, wup_ref[:, sl]), 0.0)).astype(BF16)
        part = _dot(a, wdn_ref[sl, :])
        m = part if m is None else m + part
    y_ref[...] = x1 + _rms(m, gpo_ref[...])


def _post(x, oa, ocn, layer, g_attn_out, w_o, g_post_mix, g_pre_mlp, w_up, w_down, g_post_mlp):
    n_tok, d = x.shape
    d_ff = w_up.shape[-1]
    tm = min(TOKEN_TILE, n_tok)
    assert n_tok % tm == 0 and d_ff % FF_CHUNK == 0
    row = lambda i: (i, 0)
    lay = lambda i: (layer, 0, 0)
    vec = lambda n: pl.BlockSpec((None, 1, n), lay)
    return pl.pallas_call(
        _post_kernel,
        grid=(n_tok // tm,),
        in_specs=[
            pl.BlockSpec((tm, d), row),
            pl.BlockSpec((tm, oa.shape[1]), row),
            pl.BlockSpec((tm, ocn.shape[1]), row),
            vec(oa.shape[1]),
            pl.BlockSpec((None, d, d), lay),
            vec(d), vec(d),
            pl.BlockSpec((None, d, d_ff), lay),
            pl.BlockSpec((None, d_ff, d), lay),
            vec(d),
        ],
        out_specs=pl.BlockSpec((tm, d), row),
        out_shape=jax.ShapeDtypeStruct((n_tok, d), F32),
        compiler_params=pltpu.CompilerParams(
            dimension_semantics=("arbitrary",), vmem_limit_bytes=VMEM_LIMIT_BYTES),
        name="post",
    )(x, oa, ocn, g_attn_out, w_o, g_post_mix, g_pre_mlp, w_up, w_down, g_post_mlp)


def kernel(x_prompt, x_sample, cache_k, cache_v, cache_logf, state_conv, g_pre_mix, w_in, b_f, conv_w,
           g_attn_out, g_conv_out, w_o, g_post_mix, g_pre_mlp, w_up, w_down, g_post_mlp):
    depth = w_in.shape[0]
    batch, seq, d = x_prompt.shape
    dec_batch, dec_seq, _ = x_sample.shape
    past_len = cache_k.shape[2]
    d_conv = conv_w.shape[-1]
    assert cache_k.shape[3:] == (N_HEADS, HEAD_DIM) and d_conv + ATTN_DIM == d

    w1 = _prep_w_in(w_in)
    bf_pad = jnp.pad(b_f, ((0, 0), (0, LANES - N_HEADS)))[:, None, :]
    w_o_b, w_up_b, w_dn_b = w_o.astype(BF16), w_up.astype(BF16), w_down.astype(BF16)
    vec3 = lambda a: a[:, None, :]
    g_pre_mix3, g_conv_out3, g_attn_out3 = vec3(g_pre_mix), vec3(g_conv_out), vec3(g_attn_out)
    g_post_mix3, g_pre_mlp3, g_post_mlp3 = vec3(g_post_mix), vec3(g_pre_mlp), vec3(g_post_mlp)

    ck = jnp.transpose(cache_k, (0, 1, 3, 4, 2))
    cv = jnp.transpose(cache_v, (0, 1, 3, 4, 2))
    past_lft = jnp.swapaxes(cache_logf, 2, 3)
    n_ext = past_len + CUMSUM_CHUNK * pl.cdiv(dec_seq, CUMSUM_CHUNK)
    zero_state = jnp.zeros((batch, CONV_W - 1, d_conv), F32)

    xp = x_prompt.reshape(batch * seq, d)
    xs = x_sample.reshape(dec_batch * dec_seq, d)
    outs = [[] for _ in range(8)]
    for l in range(depth):
        kp, vp, lfp, ocn_p, cst_p, qa, ka, va, stats = _proj(
            xp, l, g_pre_mix3, w1, bf_pad, conv_w, g_conv_out3, zero_state, seq_len=seq, prompt=True)
        ks, vs, lfs, ocn_s, cst_s, qz = _proj(
            xs, l, g_pre_mix3, w1, bf_pad, conv_w, g_conv_out3, state_conv[l], seq_len=dec_seq, prompt=False)
        oa_p = _attn_prompt(stats[:, :N_STATS, :N_HEADS].reshape(-1), qa, ka, va, seq_len=seq)
        new_lft = jnp.swapaxes(lfs[:, :N_HEADS].reshape(dec_batch, dec_seq, N_HEADS), 1, 2)
        lft_ext = jnp.concatenate(
            [past_lft[l], new_lft, jnp.zeros((dec_batch, N_HEADS, n_ext - past_len - dec_seq), F32)], axis=-1)
        oa_s = _attn_sample(qz, ks, vs, ck, cv, lft_ext, lfs, l, t_new=dec_seq)
        xp = _post(xp, oa_p, ocn_p, l, g_attn_out3, w_o_b, g_post_mix3, g_pre_mlp3, w_up_b, w_dn_b, g_post_mlp3)
        xs = _post(xs, oa_s, ocn_s, l, g_attn_out3, w_o_b, g_post_mix3, g_pre_mlp3, w_up_b, w_dn_b, g_post_mlp3)
        for lst, val in zip(outs, (
                kp.reshape(batch, seq, N_HEADS, HEAD_DIM), vp.reshape(batch, seq, N_HEADS, HEAD_DIM),
                lfp[:, :N_HEADS].reshape(batch, seq, N_HEADS), cst_p,
                ks.reshape(dec_batch, dec_seq, N_HEADS, HEAD_DIM), vs.reshape(dec_batch, dec_seq, N_HEADS, HEAD_DIM),
                lfs[:, :N_HEADS].reshape(dec_batch, dec_seq, N_HEADS), cst_s)):
            lst.append(val)
    return (xp.reshape(batch, seq, d), xs.reshape(dec_batch, dec_seq, d)) + tuple(jnp.stack(o) for o in outs)
```

```python
import functools
import math

import numpy as np
import jax
import jax.numpy as jnp
from jax import lax
from jax.experimental import pallas as pl
from jax.experimental.pallas import tpu as pltpu

N_HEADS = 8
HEAD_DIM = 64
ATTN_DIM = N_HEADS * HEAD_DIM
CONV_W = 3
EPS = 1e-6
NEG_INF = -1e30
LOG2E = math.log2(math.e)

LANES = 128
SUBLANES = 8
VMEM_LIMIT_BYTES = 56 * 1024 * 1024

TOKEN_TILE = 512
ATTN_QBLOCK = 1024
ATTN_KBLOCK = 1024
CUMSUM_CHUNK = 512
FF_CHUNK = 1024

F32 = jnp.float32
BF16 = jnp.bfloat16

AUG_Q0 = HEAD_DIM
AUG_K0 = HEAD_DIM
ONES_LANE_V = HEAD_DIM
CP_ONE_LANE = 3 * N_HEADS

STAT_QNORM, STAT_KNORM, STAT_OWNMIN, STAT_CFIRST, STAT_CLAST, N_STATS = 0, 1, 2, 3, 4, 5
NORM_MARGIN = 1.0 + 2.0 ** -6
SKIP_LOG2 = 160.0


def _split3(x):
    hi = x.astype(BF16).astype(F32)
    r = x - hi
    mid = r.astype(BF16).astype(F32)
    lo = (r - mid).astype(BF16).astype(F32)
    return hi, mid, lo


def _rms(x, g):
    return x * lax.rsqrt(jnp.mean(x * x, axis=-1, keepdims=True) + EPS) * g


def _log_sigmoid(x):
    return jnp.minimum(x, 0.0) - jnp.log1p(jnp.exp(-jnp.abs(x)))


def _dot(a, b):
    return jnp.dot(a, b, preferred_element_type=F32)


def _dot_nt(a, b):
    return lax.dot_general(a, b, (((1,), (1,)), ((), ())), preferred_element_type=F32)


def _placement_matrix():
    p = np.zeros((LANES, 2 * N_HEADS * LANES), np.float32)
    for h in range(N_HEADS):
        for j in range(3):
            p[j * N_HEADS + h, h * LANES + AUG_Q0 + j] = 1.0
            p[CP_ONE_LANE, h * LANES + AUG_Q0 + 3 + j] = 1.0
            p[CP_ONE_LANE, N_HEADS * LANES + h * LANES + AUG_K0 + j] = 1.0
            p[j * N_HEADS + h, N_HEADS * LANES + h * LANES + AUG_K0 + 3 + j] = -1.0
    return jnp.asarray(p, BF16)


PREP_ROWS = 256


def _prep_w_in_kernel(w_ref, o_ref):
    c0 = 3 * ATTN_DIM
    n_rest = w_ref.shape[1] - c0 - N_HEADS
    o_ref[:, 0:c0] = w_ref[:, 0:c0].astype(BF16)
    tail = w_ref[:, c0:]
    o_ref[:, c0:c0 + n_rest] = tail[:, N_HEADS:].astype(BF16)
    lane = lax.broadcasted_iota(jnp.int32, (w_ref.shape[0], LANES), 1)
    o_ref[:, c0 + n_rest:] = jnp.where(lane < N_HEADS, tail[:, 0:LANES], 0.0).astype(BF16)


def _prep_w_in(w_in):
    depth, d, n_in = w_in.shape
    n_out = n_in - N_HEADS + LANES
    assert d % PREP_ROWS == 0 and (n_in - N_HEADS) % LANES == 0
    return pl.pallas_call(
        _prep_w_in_kernel,
        grid=(depth, d // PREP_ROWS),
        in_specs=[pl.BlockSpec((None, PREP_ROWS, n_in), lambda l, i: (l, i, 0))],
        out_specs=pl.BlockSpec((None, PREP_ROWS, n_out), lambda l, i: (l, i, 0)),
        out_shape=jax.ShapeDtypeStruct((depth, d, n_out), BF16),
        compiler_params=pltpu.CompilerParams(
            dimension_semantics=("arbitrary", "arbitrary"), vmem_limit_bytes=VMEM_LIMIT_BYTES),
        name="prep_w_in",
    )(w_in)


def _proj_kernel(*refs, tm, nseq, tiles_per_seq, prompt):
    if prompt:
        (x_ref, g_ref, w_ref, bf_ref, cw_ref, gco_ref, st_ref, tri_ref, pm_ref, sel_ref,
         k_ref, v_ref, lf_ref, ocn_ref, nst_ref, qa_ref, ka_ref, va_ref, stat_ref,
         cu_scr, cc_scr) = refs
    else:
        (x_ref, g_ref, w_ref, bf_ref, cw_ref, gco_ref, st_ref,
         k_ref, v_ref, lf_ref, ocn_ref, nst_ref, qa_ref,
         cu_scr) = refs
    i = pl.program_id(0)
    d_attn = ATTN_DIM
    d_conv = cw_ref.shape[1]

    h = _rms(x_ref[...], g_ref[...]).astype(BF16)
    z = _dot(h, w_ref[...])
    zq = z[:, 0:d_attn] * (HEAD_DIM ** -0.5 * LOG2E)
    zk = z[:, d_attn:2 * d_attn]
    zv = z[:, 2 * d_attn:3 * d_attn]
    o = 3 * d_attn
    gate_b = z[:, o:o + d_conv]
    cu = z[:, o + d_conv:o + 2 * d_conv] * z[:, o + 2 * d_conv:o + 3 * d_conv]
    fz = z[:, o + 3 * d_conv:o + 3 * d_conv + LANES] + bf_ref[...]

    k_ref[...] = zk
    v_ref[...] = zv
    lane = lax.broadcasted_iota(jnp.int32, (tm, LANES), 1)
    logf = jnp.where(lane < N_HEADS, _log_sigmoid(fz), 0.0)
    lf_ref[...] = logf

    seg = tm // nseq
    slot = seg + SUBLANES
    if nseq == 1:
        first = (i % tiles_per_seq) == 0

        @pl.when(first)
        def _():
            cu_scr[6:8, :] = st_ref[0]

        @pl.when(jnp.logical_not(first))
        def _():
            cu_scr[6:8, :] = cu_scr[seg + 6:seg + 8, :]

        cu_scr[8:8 + seg, :] = cu
        cu_m1 = cu_scr[7:7 + seg, :]
        cu_m2 = cu_scr[6:6 + seg, :]
        nst_ref[0] = cu_scr[seg + 6:seg + 8, :]
    else:
        for s in range(nseq):
            cu_scr[s * slot + 6:s * slot + 8, :] = st_ref[s]
            cu_scr[s * slot + 8:s * slot + 8 + seg, :] = cu[s * seg:(s + 1) * seg, :]
        cu_m1 = jnp.concatenate([cu_scr[s * slot + 7:s * slot + 7 + seg, :] for s in range(nseq)], axis=0)
        cu_m2 = jnp.concatenate([cu_scr[s * slot + 6:s * slot + 6 + seg, :] for s in range(nseq)], axis=0)
        for s in range(nseq):
            nst_ref[s] = cu_scr[s * slot + seg + 6:s * slot + seg + 8, :]
    cw = cw_ref[...]
    y = cw[0:1, :] * cu_m2 + cw[1:2, :] * cu_m1 + cw[2:3, :] * cu
    ocn_ref[...] = _rms(gate_b * y, gco_ref[...]).astype(BF16)

    if prompt:
        tri = tri_ref[...]
        hi, mid, lo = _split3(logf)
        loc = _dot(tri, hi.astype(BF16)) + _dot(tri, mid.astype(BF16)) + _dot(tri, lo.astype(BF16))

        @pl.when((i % tiles_per_seq) == 0)
        def _():
            cc_scr[...] = jnp.zeros_like(cc_scr)

        c = loc + cc_scr[SUBLANES - 1:SUBLANES, :]
        cc_scr[...] = c[tm - SUBLANES:tm, :]
        c2 = c * LOG2E
        chi, cmid, clo = _split3(c2)
        cp = (chi + pltpu.roll(cmid, N_HEADS, 1) + pltpu.roll(clo, 2 * N_HEADS, 1)
              + (lane == CP_ONE_LANE).astype(F32))
        aug = _dot(cp.astype(BF16), pm_ref[...])
        ones_v = (lane == ONES_LANE_V).astype(F32)

        zqr = zq.astype(BF16).astype(F32)
        zkr = zk.astype(BF16).astype(F32)
        head_sum = lambda prod: _dot(prod.astype(BF16), sel_ref[...])
        qn = jnp.sqrt(jnp.max(head_sum(zqr * zqr), axis=0, keepdims=True)) * NORM_MARGIN
        kn = jnp.sqrt(jnp.max(head_sum(zkr * zkr), axis=0, keepdims=True)) * NORM_MARGIN
        own = jnp.min(head_sum(zqr * zkr), axis=0, keepdims=True) - (NORM_MARGIN - 1.0) * qn * kn
        srow = lax.broadcasted_iota(jnp.int32, (SUBLANES, LANES), 0)
        stat_ref[0] = jnp.where(
            srow == STAT_QNORM, qn, jnp.where(
                srow == STAT_KNORM, kn, jnp.where(
                    srow == STAT_OWNMIN, own, jnp.where(
                        srow == STAT_CFIRST, c2[0:1, :], jnp.where(
                            srow == STAT_CLAST, c2[tm - 1:tm, :], 0.0)))))

    low = lane < HEAD_DIM
    for j in range(N_HEADS // 2):
        qb = zq[:, j * LANES:(j + 1) * LANES]
        if prompt:
            kb = zk[:, j * LANES:(j + 1) * LANES]
            vb = zv[:, j * LANES:(j + 1) * LANES]
        for par in range(2):
            hd = 2 * j + par
            if prompt:
                qs, ks, vs = (qb, kb, vb) if par == 0 else (
                    pltpu.roll(qb, HEAD_DIM, 1), pltpu.roll(kb, HEAD_DIM, 1), pltpu.roll(vb, HEAD_DIM, 1))
                qa_ref[hd] = jnp.where(low, qs, aug[:, hd * LANES:(hd + 1) * LANES]).astype(BF16)
                ka_ref[hd] = jnp.where(
                    low, ks, aug[:, (N_HEADS + hd) * LANES:(N_HEADS + hd + 1) * LANES]).astype(BF16)
                va_ref[hd] = jnp.where(low, vs, ones_v).T.astype(BF16)
            else:
                qs = qb if par == 0 else pltpu.roll(qb, HEAD_DIM, 1)
                qa_ref[hd] = jnp.where(low, qs, 0.0).astype(BF16)


def _proj(x, layer, g_pre_mix, w1, bf_pad, conv_w, g_conv_out, state, *, seq_len, prompt):
    n_tok, d = x.shape
    d_conv = conv_w.shape[-1]
    n_cols = w1.shape[-1]
    tm = min(TOKEN_TILE, n_tok)
    assert n_tok % tm == 0
    if seq_len >= tm:
        assert seq_len % tm == 0
        nseq, tiles_per_seq = 1, seq_len // tm
        st_map = lambda i: (i // tiles_per_seq, 0, 0)
    else:
        assert tm % seq_len == 0 and seq_len % SUBLANES == 0
        nseq, tiles_per_seq = tm // seq_len, 1
        st_map = lambda i: (i, 0, 0)
    seg = tm // nseq
    n_tiles = n_tok // tm
    row = lambda i: (i, 0)
    lay2 = lambda i: (layer, 0, 0)
    in_specs = [
        pl.BlockSpec((tm, d), row),
        pl.BlockSpec((None, 1, d), lay2),
        pl.BlockSpec((None, d, n_cols), lay2),
        pl.BlockSpec((None, 1, LANES), lay2),
        pl.BlockSpec((None, CONV_W, d_conv), lay2),
        pl.BlockSpec((None, 1, d_conv), lay2),
        pl.BlockSpec((nseq, CONV_W - 1, d_conv), st_map),
    ]
    args = [x, g_pre_mix, w1, bf_pad, conv_w, g_conv_out, state]
    head_spec = pl.BlockSpec((N_HEADS, tm, LANES), lambda i: (0, i, 0))
    out_specs = [
        pl.BlockSpec((tm, ATTN_DIM), row),
        pl.BlockSpec((tm, ATTN_DIM), row),
        pl.BlockSpec((tm, LANES), row),
        pl.BlockSpec((tm, d_conv), row),
        pl.BlockSpec((nseq, CONV_W - 1, d_conv), st_map),
        head_spec,
    ]
    out_shape = [
        jax.ShapeDtypeStruct((n_tok, ATTN_DIM), F32),
        jax.ShapeDtypeStruct((n_tok, ATTN_DIM), F32),
        jax.ShapeDtypeStruct((n_tok, LANES), F32),
        jax.ShapeDtypeStruct((n_tok, d_conv), BF16),
        jax.ShapeDtypeStruct(state.shape, F32),
        jax.ShapeDtypeStruct((N_HEADS, n_tok, LANES), BF16),
    ]
    scratch = [pltpu.VMEM((nseq * (seg + SUBLANES), d_conv), F32)]
    if prompt:
        tri = jnp.tril(jnp.ones((tm, tm), BF16))
        head_sel = jnp.asarray(
            np.arange(ATTN_DIM)[:, None] // HEAD_DIM == np.arange(LANES)[None, :], BF16)
        in_specs += [pl.BlockSpec((tm, tm), lambda i: (0, 0)),
                     pl.BlockSpec((LANES, 2 * N_HEADS * LANES), lambda i: (0, 0)),
                     pl.BlockSpec((ATTN_DIM, LANES), lambda i: (0, 0))]
        args += [tri, _placement_matrix(), head_sel]
        out_specs += [head_spec, pl.BlockSpec((N_HEADS, LANES, tm), lambda i: (0, 0, i)),
                      pl.BlockSpec((1, SUBLANES, LANES), lambda i: (i, 0, 0))]
        out_shape += [jax.ShapeDtypeStruct((N_HEADS, n_tok, LANES), BF16),
                      jax.ShapeDtypeStruct((N_HEADS, LANES, n_tok), BF16)]
        out_shape += [jax.ShapeDtypeStruct((n_tiles, SUBLANES, LANES), F32)]
        scratch += [pltpu.VMEM((SUBLANES, LANES), F32)]
    kern = functools.partial(_proj_kernel, tm=tm, nseq=nseq, tiles_per_seq=tiles_per_seq, prompt=prompt)
    return pl.pallas_call(
        kern,
        grid=(n_tiles,),
        in_specs=in_specs,
        out_specs=out_specs,
        out_shape=out_shape,
        scratch_shapes=scratch,
        compiler_params=pltpu.CompilerParams(
            dimension_semantics=("arbitrary",), vmem_limit_bytes=VMEM_LIMIT_BYTES),
        name="proj_prompt" if prompt else "proj_sample",
    )(*args)


def _attn_prompt_kernel(stat_ref, q_ref, k_ref, v_ref, o_ref, p_scr, al_scr, m_scr, acc_scr, *, tq, tk, nbq, tile):
    pair = pl.program_id(0)
    qi = pl.program_id(1) % nbq
    r = tq // tk
    q_tiles, k_tiles = tq // tile, tk // tile
    tile0 = (pl.program_id(1) // nbq) * (nbq * q_tiles)
    lane = lax.broadcasted_iota(jnp.int32, (tq, LANES), 1)
    j_diag = qi * r

    def first_block(hd):
        def stat(t, which):
            return stat_ref[((tile0 + t) * N_STATS + which) * N_HEADS + hd]

        q_norm = functools.reduce(jnp.maximum, [stat(qi * q_tiles + a, STAT_QNORM) for a in range(q_tiles)])
        own_min = functools.reduce(jnp.minimum, [stat(qi * q_tiles + a, STAT_OWNMIN) for a in range(q_tiles)])
        k_all = lax.fori_loop(
            0, nbq * q_tiles, lambda t, acc: jnp.maximum(acc, stat(t, STAT_KNORM)), jnp.float32(0.0))
        reach = q_norm * k_all - own_min + stat(qi * q_tiles, STAT_CFIRST) + SKIP_LOG2
        return lax.while_loop(
            lambda j: jnp.logical_and(
                j > 0, reach - stat(jnp.maximum(j, 1) * k_tiles - 1, STAT_CLAST) >= 0.0),
            lambda j: j - 1, j_diag)

    j_first = [first_block(2 * pair + hh) for hh in range(2)]

    dT = (lax.broadcasted_iota(jnp.int32, (tk, tq), 0) - lax.broadcasted_iota(jnp.int32, (tk, tq), 1))

    def softmax_stage(hh, j, row0, diagonal):
        start = pl.multiple_of(j * tk, tk)
        s = _dot_nt(k_ref[hh, pl.ds(start, tk), :], q_ref[hh, row0:, :])
        if diagonal:
            s = jnp.where(dT[:, row0:] <= qi * tq - j * tk, s, NEG_INF)
        m_old = m_scr[hh, :, row0:]
        m_new = jnp.maximum(m_old, jnp.max(s, axis=0, keepdims=True))
        m_scr[hh, :, row0:] = m_new
        p_scr[hh, :, row0:] = jnp.exp2(s - m_new[0:1, :]).astype(BF16)
        al_scr[hh, :, row0:] = jnp.exp2(m_old - m_new)

    def pv_stage(hh, p, alpha, j, row0):
        start = pl.multiple_of(j * tk, tk)
        acc_scr[hh, :, row0:] = alpha[0:1, :] * acc_scr[hh, :, row0:] + _dot(v_ref[hh, :, pl.ds(start, tk)], p)

    def step(hh, j, row0_prev, row0, diagonal):
        p_prev, al_prev = p_scr[hh, :, row0_prev:], al_scr[hh, :, row0_prev:]
        softmax_stage(hh, j - 1, row0, diagonal)
        pv_stage(hh, p_prev, al_prev, j, row0_prev)

    m_scr[...] = jnp.full(m_scr.shape, NEG_INF, F32)
    acc_scr[...] = jnp.zeros_like(acc_scr)
    for hh in range(2):
        softmax_stage(hh, j_diag + r - 1, (r - 1) * tk, True)
    for a in range(r - 1, 0, -1):
        for hh in range(2):
            step(hh, j_diag + a, a * tk, (a - 1) * tk, True)
    for hh in range(2):
        def body(t, carry, hh=hh):
            step(hh, j_diag - t, 0, 0, False)
            return carry

        lax.fori_loop(0, j_diag - j_first[hh], body, 0)
    outs = []
    for hh in range(2):
        pv_stage(hh, p_scr[hh], al_scr[hh], j_first[hh], 0)
        acc = acc_scr[hh]
        outs.append((acc / acc[ONES_LANE_V:ONES_LANE_V + 1, :]).T)
    o_ref[...] = jnp.where(lane < HEAD_DIM, outs[0], pltpu.roll(outs[1], HEAD_DIM, 1))


def _attn_prompt(stats, qa, ka, va, *, seq_len):
    n_tok = qa.shape[1]
    tile = min(TOKEN_TILE, seq_len)
    tq = min(ATTN_QBLOCK, seq_len)
    tk = min(ATTN_KBLOCK, seq_len)
    assert seq_len % tq == 0 and tq % tk == 0 and tk % tile == 0 and n_tok % seq_len == 0
    n_seq = n_tok // seq_len
    nbq = seq_len // tq
    kv_spec = pl.BlockSpec((2, seq_len, LANES), lambda p, i: (p, i // nbq, 0))
    vt_spec = pl.BlockSpec((2, LANES, seq_len), lambda p, i: (p, 0, i // nbq))
    return pl.pallas_call(
        functools.partial(_attn_prompt_kernel, tq=tq, tk=tk, nbq=nbq, tile=tile),
        grid=(N_HEADS // 2, n_seq * nbq),
        in_specs=[pl.BlockSpec(memory_space=pltpu.SMEM),
                  pl.BlockSpec((2, tq, LANES), lambda p, i: (p, i, 0)), kv_spec, vt_spec],
        out_specs=pl.BlockSpec((tq, LANES), lambda p, i: (i, p)),
        out_shape=jax.ShapeDtypeStruct((n_tok, ATTN_DIM), F32),
        scratch_shapes=[pltpu.VMEM((2, tk, tq), BF16), pltpu.VMEM((2, SUBLANES, tq), F32), pltpu.VMEM((2, SUBLANES, tq), F32), pltpu.VMEM((2, LANES, tq), F32)],
        compiler_params=pltpu.CompilerParams(
            dimension_semantics=("arbitrary", "arbitrary"), vmem_limit_bytes=VMEM_LIMIT_BYTES),
        name="attn_prompt",
    )(stats, qa, ka, va)


def _attn_sample_kernel(qz_ref, kn_ref, vn_ref, kc_ref, vc_ref, lft_ref, lf_ref, u_ref, o_ref, *, t_new, p_len):
    n_ext = lft_ref.shape[-1]
    n_chunks = n_ext // CUMSUM_CHUNK
    u = u_ref[...]

    hi, mid, lo = _split3(lft_ref[...])
    x3 = jnp.concatenate([hi, mid, lo, jnp.zeros_like(hi)], axis=0).astype(BF16)
    offs = jnp.zeros((N_HEADS, 1), F32)
    pieces = []
    for c in range(n_chunks):
        loc3 = _dot(x3[:, c * CUMSUM_CHUNK:(c + 1) * CUMSUM_CHUNK], u)
        loc = loc3[0:N_HEADS] + loc3[N_HEADS:2 * N_HEADS] + loc3[2 * N_HEADS:3 * N_HEADS]
        pieces.append(loc + offs)
        offs = offs + loc[:, CUMSUM_CHUNK - 1:CUMSUM_CHUNK]
    c_abs = jnp.concatenate(pieces, axis=1)
    p_tot = c_abs[:, p_len - 1:p_len]
    r2 = (p_tot - c_abs) * LOG2E

    rown = lax.broadcasted_iota(jnp.int32, (t_new, t_new), 0)
    coln = lax.broadcasted_iota(jnp.int32, (t_new, t_new), 1)
    tri = (coln <= rown).astype(BF16)
    nhi, nmid, nlo = _split3(lf_ref[...])
    cn2 = (_dot(tri, nhi.astype(BF16)) + _dot(tri, nmid.astype(BF16)) + _dot(tri, nlo.astype(BF16))) * LOG2E

    kn = kn_ref[...].astype(BF16)
    vn = vn_ref[...].astype(BF16)
    for hd in range(N_HEADS):
        sl = slice(hd * HEAD_DIM, (hd + 1) * HEAD_DIM)
        q = qz_ref[hd][:, :HEAD_DIM]
        cq = cn2[:, hd:hd + 1]
        sp = _dot(q, kc_ref[hd].astype(BF16)) + cq + r2[hd:hd + 1, 0:p_len]
        sn = _dot_nt(q, kn[:, sl]) + cq + r2[hd:hd + 1, p_len:p_len + t_new]
        sn = jnp.where(coln <= rown, sn, NEG_INF)
        m = jnp.maximum(jnp.max(sp, axis=-1, keepdims=True), jnp.max(sn, axis=-1, keepdims=True))
        pp = jnp.exp2(sp - m)
        pn = jnp.exp2(sn - m)
        l = jnp.sum(pp, axis=-1, keepdims=True) + jnp.sum(pn, axis=-1, keepdims=True)
        o_ref[:, sl] = (_dot_nt(pp.astype(BF16), vc_ref[hd].astype(BF16)) + _dot(pn.astype(BF16), vn[:, sl])) / l


def _attn_sample(qz, k_new, v_new, cache_kt, cache_vt, lft_ext, lf_pad, layer, *, t_new):
    n_tok = k_new.shape[0]
    n_b = n_tok // t_new
    p_len = cache_kt.shape[-1]
    n_ext = lft_ext.shape[-1]
    assert p_len % CUMSUM_CHUNK == 0 and n_ext % CUMSUM_CHUNK == 0 and t_new % SUBLANES == 0
    u = jnp.triu(jnp.ones((CUMSUM_CHUNK, CUMSUM_CHUNK), BF16))
    tok = lambda b: (b, 0)
    cache_spec = pl.BlockSpec((None, None, N_HEADS, HEAD_DIM, p_len), lambda b: (layer, b, 0, 0, 0))
    return pl.pallas_call(
        functools.partial(_attn_sample_kernel, t_new=t_new, p_len=p_len),
        grid=(n_b,),
        in_specs=[
            pl.BlockSpec((N_HEADS, t_new, LANES), lambda b: (0, b, 0)),
            pl.BlockSpec((t_new, ATTN_DIM), tok),
            pl.BlockSpec((t_new, ATTN_DIM), tok),
            cache_spec, cache_spec,
            pl.BlockSpec((None, N_HEADS, n_ext), lambda b: (b, 0, 0)),
            pl.BlockSpec((t_new, LANES), tok),
            pl.BlockSpec((CUMSUM_CHUNK, CUMSUM_CHUNK), lambda b: (0, 0)),
        ],
        out_specs=pl.BlockSpec((t_new, ATTN_DIM), tok),
        out_shape=jax.ShapeDtypeStruct((n_tok, ATTN_DIM), F32),
        compiler_params=pltpu.CompilerParams(
            dimension_semantics=("arbitrary",), vmem_limit_bytes=VMEM_LIMIT_BYTES),
        name="attn_sample",
    )(qz, k_new, v_new, cache_kt, cache_vt, lft_ext, lf_pad, u)


def _post_kernel(x_ref, oa_ref, ocn_ref, gao_ref, wo_ref, gpm_ref, gpl_ref, wup_ref, wdn_ref, gpo_ref, y_ref):
    d_ff = wup_ref.shape[1]
    oan = _rms(oa_ref[...], gao_ref[...]).astype(BF16)
    merged = jnp.concatenate([oan, ocn_ref[...]], axis=-1)
    x1 = x_ref[...] + _rms(_dot(merged, wo_ref[...]), gpm_ref[...])
    h2 = _rms(x1, gpl_ref[...]).astype(BF16)
    m = None
    for c in range(d_ff // FF_CHUNK):
        sl = slice(c * FF_CHUNK, (c + 1) * FF_CHUNK)
        a = jnp.square(jnp.maximum(_dot(h2, wup_ref[:, sl]), 0.0)).astype(BF16)
        part = _dot(a, wdn_ref[sl, :])
        m = part if m is None else m + part
    y_ref[...] = x1 + _rms(m, gpo_ref[...])


def _post(x, oa, ocn, layer, g_attn_out, w_o, g_post_mix, g_pre_mlp, w_up, w_down, g_post_mlp):
    n_tok, d = x.shape
    d_ff = w_up.shape[-1]
    tm = min(TOKEN_TILE, n_tok)
    assert n_tok % tm == 0 and d_ff % FF_CHUNK == 0
    row = lambda i: (i, 0)
    lay = lambda i: (layer, 0, 0)
    vec = lambda n: pl.BlockSpec((None, 1, n), lay)
    return pl.pallas_call(
        _post_kernel,
        grid=(n_tok // tm,),
        in_specs=[
            pl.BlockSpec((tm, d), row),
            pl.BlockSpec((tm, oa.shape[1]), row),
            pl.BlockSpec((tm, ocn.shape[1]), row),
            vec(oa.shape[1]),
            pl.BlockSpec((None, d, d), lay),
            vec(d), vec(d),
            pl.BlockSpec((None, d, d_ff), lay),
            pl.BlockSpec((None, d_ff, d), lay),
            vec(d),
        ],
        out_specs=pl.BlockSpec((tm, d), row),
        out_shape=jax.ShapeDtypeStruct((n_tok, d), F32),
        compiler_params=pltpu.CompilerParams(
            dimension_semantics=("arbitrary",), vmem_limit_bytes=VMEM_LIMIT_BYTES),
        name="post",
    )(x, oa, ocn, g_attn_out, w_o, g_post_mix, g_pre_mlp, w_up, w_down, g_post_mlp)


def kernel(x_prompt, x_sample, cache_k, cache_v, cache_logf, state_conv, g_pre_mix, w_in, b_f, conv_w,
           g_attn_out, g_conv_out, w_o, g_post_mix, g_pre_mlp, w_up, w_down, g_post_mlp):
    depth = w_in.shape[0]
    batch, seq, d = x_prompt.shape
    dec_batch, dec_seq, _ = x_sample.shape
    past_len = cache_k.shape[2]
    d_conv = conv_w.shape[-1]
    assert cache_k.shape[3:] == (N_HEADS, HEAD_DIM) and d_conv + ATTN_DIM == d

    w1 = _prep_w_in(w_in)
    bf_pad = jnp.pad(b_f, ((0, 0), (0, LANES - N_HEADS)))[:, None, :]
    w_o_b, w_up_b, w_dn_b = w_o.astype(BF16), w_up.astype(BF16), w_down.astype(BF16)
    vec3 = lambda a: a[:, None, :]
    g_pre_mix3, g_conv_out3, g_attn_out3 = vec3(g_pre_mix), vec3(g_conv_out), vec3(g_attn_out)
    g_post_mix3, g_pre_mlp3, g_post_mlp3 = vec3(g_post_mix), vec3(g_pre_mlp), vec3(g_post_mlp)

    ck = jnp.transpose(cache_k, (0, 1, 3, 4, 2))
    cv = jnp.transpose(cache_v, (0, 1, 3, 4, 2))
    past_lft = jnp.swapaxes(cache_logf, 2, 3)
    n_ext = past_len + CUMSUM_CHUNK * pl.cdiv(dec_seq, CUMSUM_CHUNK)
    zero_state = jnp.zeros((batch, CONV_W - 1, d_conv), F32)

    xp = x_prompt.reshape(batch * seq, d)
    xs = x_sample.reshape(dec_batch * dec_seq, d)
    outs = [[] for _ in range(8)]
    for l in range(depth):
        kp, vp, lfp, ocn_p, cst_p, qa, ka, va, stats = _proj(
            xp, l, g_pre_mix3, w1, bf_pad, conv_w, g_conv_out3, zero_state, seq_len=seq, prompt=True)
        ks, vs, lfs, ocn_s, cst_s, qz = _proj(
            xs, l, g_pre_mix3, w1, bf_pad, conv_w, g_conv_out3, state_conv[l], seq_len=dec_seq, prompt=False)
        oa_p = _attn_prompt(stats[:, :N_STATS, :N_HEADS].reshape(-1), qa, ka, va, seq_len=seq)
        new_lft = jnp.swapaxes(lfs[:, :N_HEADS].reshape(dec_batch, dec_seq, N_HEADS), 1, 2)
        lft_ext = jnp.concatenate(
            [past_lft[l], new_lft, jnp.zeros((dec_batch, N_HEADS, n_ext - past_len - dec_seq), F32)], axis=-1)
        oa_s = _attn_sample(qz, ks, vs, ck, cv, lft_ext, lfs, l, t_new=dec_seq)
        xp = _post(xp, oa_p, ocn_p, l, g_attn_out3, w_o_b, g_post_mix3, g_pre_mlp3, w_up_b, w_dn_b, g_post_mlp3)
        xs = _post(xs, oa_s, ocn_s, l, g_attn_out3, w_o_b, g_post_mix3, g_pre_mlp3, w_up_b, w_dn_b, g_post_mlp3)
        for lst, val in zip(outs, (
                kp.reshape(batch, seq, N_HEADS, HEAD_DIM), vp.reshape(batch, seq, N_HEADS, HEAD_DIM),
                lfp[:, :N_HEADS].reshape(batch, seq, N_HEADS), cst_p,
                ks.reshape(dec_batch, dec_seq, N_HEADS, HEAD_DIM), vs.reshape(dec_batch, dec_seq, N_HEADS, HEAD_DIM),
                lfs[:, :N_HEADS].reshape(dec_batch, dec_seq, N_HEADS), cst_s)):
            lst.append(val)
    return (xp.reshape(batch, seq, d), xs.reshape(dec_batch, dec_seq, d)) + tuple(jnp.stack(o) for o in outs)
```
